```python
import math
import jax
import jax.numpy as jnp
from jax import lax
import numpy as np

D_MODEL = 2048
BATCH = 4
SEQ = 2048
DEPTH = 4
DEC_BATCH = 128
DEC_SEQ = 4
PAST_LEN = 16384
PAGE_SIZE = 128

N_MIXERS = 3
N_A = len(range(0, DEPTH, N_MIXERS))
N_B = len(range(1, DEPTH, N_MIXERS))
N_C = len(range(2, DEPTH, N_MIXERS))
RMS_EPS = 1e-6
SSM_EXPAND = 2
SSM_INNER = SSM_EXPAND * D_MODEL
SSM_HEAD_DIM = 64
SSM_HEADS = SSM_INNER // SSM_HEAD_DIM
SSM_GROUPS = 8
SSM_HPG = SSM_HEADS // SSM_GROUPS
SSM_STATE = 128
SSM_CONV = 4
SSM_CONV_DIM = SSM_INNER + 2 * SSM_GROUPS * SSM_STATE
SSM_IN_DIM = SSM_INNER + SSM_CONV_DIM + SSM_HEADS
SSM_CHUNK = 128
LRU_WIDTH = D_MODEL
LRU_HEADS = 16
LRU_BLOCK = LRU_WIDTH // LRU_HEADS
LRU_CONV = 4
LRU_C = 8.0
SC_WIDTH = 3
MOE_GROUPS = 8
MOE_PER_GROUP = 4
N_EXPERTS = MOE_GROUPS * MOE_PER_GROUP
MOE_TOPK = 2
MOE_FF = 512

kernel_name = 'hybrid_ssd_rglru_shortconv_hmoe_step'


def rmsnorm(x, w):
    xf = x.astype(jnp.float32)
    xf = xf * lax.rsqrt(jnp.mean(xf * xf, axis=-1, keepdims=True) + RMS_EPS)
    return (xf * w.astype(jnp.float32)).astype(x.dtype)


def causal_dwconv(x, prev, w, b=None):
    width = w.shape[0]
    T = x.shape[1]
    xp = jnp.concatenate([prev.astype(x.dtype), x], axis=1)
    y = xp[:, 0:T] * w[0]
    for k in range(1, width):
        y = y + xp[:, k:k + T] * w[k]
    if b is not None:
        y = y + b
    return y, xp[:, T:]


def ssd_chunked(xs, dt, A, Bm, Cm, h0):
    b, T = xs.shape[:2]
    L = SSM_CHUNK if T % SSM_CHUNK == 0 else T
    c = T // L

    def chunks(t):
        return t.reshape((b, c, L) + t.shape[2:])

    Xd = chunks(xs * dt[..., None])
    Ad = chunks(dt * A)
    Bc = chunks(Bm)
    Cc = chunks(Cm)
    Acs = jnp.cumsum(Ad, axis=2)
    cs = jnp.moveaxis(Acs, 2, -1)
    causal = jnp.tril(jnp.ones((L, L), dtype=bool))
    decay = jnp.exp(jnp.where(causal, cs[..., :, None] - cs[..., None, :], -jnp.inf))
    cb = jnp.einsum('bclgn,bcsgn->bcgls', Cc, Bc)
    y_diag = jnp.einsum('bcgrls,bcsgrp->bclgrp', cb[:, :, :, None] * decay, Xd)
    to_end = jnp.exp(Acs[:, :, -1:] - Acs)
    chunk_states = jnp.einsum('bclgn,bclgrp->bcgrpn', Bc, Xd * to_end[..., None])
    chunk_decay = jnp.exp(Acs[:, :, -1])

    def step(h, inp):
        dec, st = inp
        return h * dec[..., None, None] + st, h

    hT, h_in = lax.scan(step, h0.astype(jnp.float32),
                        (jnp.moveaxis(chunk_decay, 1, 0), jnp.moveaxis(chunk_states, 1, 0)))
    h_in = jnp.moveaxis(h_in, 0, 1)
    y_off = jnp.einsum('bclgn,bcgrpn->bclgrp', Cc, h_in) * jnp.exp(Acs)[..., None]
    return (y_diag + y_off).reshape(xs.shape), hT


def mamba2_mixer(x, conv_prev, h0, w_in, conv_w, conv_b, dt_bias, A_log, d_skip, norm_w, w_out):
    b, T, _ = x.shape
    proj = x @ w_in
    z, xbc, dt_raw = jnp.split(proj, [SSM_INNER, SSM_INNER + SSM_CONV_DIM], axis=-1)
    xbc, conv_new = causal_dwconv(xbc, conv_prev, conv_w, conv_b)
    xbc = jax.nn.silu(xbc)
    xs, Bm, Cm = jnp.split(xbc, [SSM_INNER, SSM_INNER + SSM_GROUPS * SSM_STATE], axis=-1)
    xs = xs.reshape(b, T, SSM_GROUPS, SSM_HPG, SSM_HEAD_DIM)
    Bm = Bm.reshape(b, T, SSM_GROUPS, SSM_STATE)
    Cm = Cm.reshape(b, T, SSM_GROUPS, SSM_STATE)
    dt = jax.nn.softplus(dt_raw.astype(jnp.float32) + dt_bias.astype(jnp.float32))
    dt = dt.reshape(b, T, SSM_GROUPS, SSM_HPG)
    A = -jnp.exp(A_log.astype(jnp.float32)).reshape(SSM_GROUPS, SSM_HPG)
    h0 = h0.reshape(b, SSM_GROUPS, SSM_HPG, SSM_HEAD_DIM, SSM_STATE)
    y, hT = ssd_chunked(xs, dt, A, Bm, Cm, h0)
    y = y + xs * d_skip.reshape(SSM_GROUPS, SSM_HPG)[:, :, None]
    y = y.reshape(b, T, SSM_INNER) * jax.nn.silu(z)
    yg = y.reshape(b, T, SSM_GROUPS, SSM_INNER // SSM_GROUPS).astype(jnp.float32)
    yg = yg * lax.rsqrt(jnp.mean(yg * yg, axis=-1, keepdims=True) + RMS_EPS)
    y = (yg.reshape(b, T, SSM_INNER) * norm_w.astype(jnp.float32)).astype(x.dtype)
    return y @ w_out, conv_new, hT.reshape(b, SSM_HEADS, SSM_HEAD_DIM, SSM_STATE)


def rglru_mixer(x, conv_prev, h0, w_in, conv_w, conv_b, w_a, b_a, w_i, b_i, lam, w_out):
    b, T, _ = x.shape
    gate_in, xr = jnp.split(x @ w_in, 2, axis=-1)
    gate = jax.nn.gelu(gate_in)
    xr, conv_new = causal_dwconv(xr, conv_prev, conv_w, conv_b)
    xh = xr.reshape(b, T, LRU_HEADS, LRU_BLOCK)
    r = jax.nn.sigmoid(jnp.einsum('bthi,hij->bthj', xh, w_a).reshape(b, T, LRU_WIDTH) + b_a)
    ig = jax.nn.sigmoid(jnp.einsum('bthi,hij->bthj', xh, w_i).reshape(b, T, LRU_WIDTH) + b_i)
    log_a = -LRU_C * r.astype(jnp.float32) * jax.nn.softplus(-lam.astype(jnp.float32))
    a = jnp.exp(log_a)
    mult = jnp.sqrt(-jnp.expm1(2.0 * log_a))
    u = mult * (ig * xr).astype(jnp.float32)
    u = u.at[:, 0].add(a[:, 0] * h0.astype(jnp.float32))

    def combine(left, right):
        a1, b1 = left
        a2, b2 = right
        return a1 * a2, a2 * b1 + b2

    _, h = lax.associative_scan(combine, (a, u), axis=1)
    y = (h.astype(x.dtype) * gate) @ w_out
    return y, conv_new, h[:, -1]


def shortconv_mixer(x, conv_prev, w_in, conv_w, w_out):
    bg, cg, h = jnp.split(x @ w_in, 3, axis=-1)
    u, conv_new = causal_dwconv(cg * h, conv_prev, conv_w)
    return (bg * u) @ w_out, conv_new


def hier_moe(x, w_grp, w_exp, w_gate, w_up, w_down):
    b, T, D = x.shape
    xt = x.reshape(-1, D)
    g_logits = (xt @ w_grp).astype(jnp.float32)
    g_prob = jax.nn.softmax(g_logits, axis=-1)
    g_idx = jnp.argmax(g_logits, axis=-1)
    g_w = jnp.take_along_axis(g_prob, g_idx[:, None], axis=-1)
    e_logits = (xt @ w_exp).astype(jnp.float32).reshape(-1, MOE_GROUPS, MOE_PER_GROUP)
    e_logits = jnp.take_along_axis(e_logits, g_idx[:, None, None], axis=1)[:, 0]
    top_v, top_i = lax.top_k(e_logits, MOE_TOPK)
    top_w = jax.nn.softmax(top_v, axis=-1) * g_w
    expert_id = g_idx[:, None] * MOE_PER_GROUP + top_i
    comb = jnp.sum(jax.nn.one_hot(expert_id, N_EXPERTS, dtype=jnp.float32) * top_w[..., None], axis=1)
    hg = jnp.einsum('nd,edf->nef', xt, w_gate)
    hu = jnp.einsum('nd,edf->nef', xt, w_up)
    hid = jax.nn.silu(hg) * hu * comb[..., None].astype(hu.dtype)
    y = jnp.einsum('nef,efd->nd', hid, w_down)
    return y.reshape(b, T, D)


def run_trunk(x, ssm_conv, ssm_h, lru_conv, lru_h, sc_conv, p):
    n_ssm_conv, n_ssm, n_lru_conv, n_lru, n_sc = [], [], [], [], []
    for i in range(DEPTH):
        kind = i % N_MIXERS
        s = i // N_MIXERS
        hn = rmsnorm(x, p['norm_mix'][i])
        if kind == 0:
            out, c_new, h_new = mamba2_mixer(hn, ssm_conv[s], ssm_h[s], p['w_ssm_in'][s], p['ssm_conv_w'][s],
                                             p['ssm_conv_b'][s], p['ssm_dt_bias'][s], p['ssm_A_log'][s],
                                             p['ssm_D'][s], p['ssm_norm_w'][s], p['w_ssm_out'][s])
            n_ssm_conv.append(c_new)
            n_ssm.append(h_new)
        elif kind == 1:
            out, c_new, h_new = rglru_mixer(hn, lru_conv[s], lru_h[s], p['w_lru_in'][s], p['lru_conv_w'][s],
                                            p['lru_conv_b'][s], p['lru_wa'][s], p['lru_ba'][s], p['lru_wi'][s],
                                            p['lru_bi'][s], p['lru_lambda'][s], p['w_lru_out'][s])
            n_lru_conv.append(c_new)
            n_lru.append(h_new)
        else:
            out, c_new = shortconv_mixer(hn, sc_conv[s], p['w_sc_in'][s], p['sc_conv_w'][s], p['w_sc_out'][s])
            n_sc.append(c_new)
        x = x + out.astype(x.dtype)
        f = hier_moe(rmsnorm(x, p['norm_ffn'][i]), p['w_route_group'][i], p['w_route_expert'][i],
                     p['w_exp_gate'][i], p['w_exp_up'][i], p['w_exp_down'][i])
        x = x + f.astype(x.dtype)
    y = rmsnorm(x, p['norm_final'])
    return y, (jnp.stack(n_ssm_conv), jnp.stack(n_ssm), jnp.stack(n_lru_conv), jnp.stack(n_lru), jnp.stack(n_sc))


def setup_inputs(seed: int = 0) -> dict:
    key = jax.random.key(seed)
    ks = iter(jax.random.split(key, 64))

    def nrm(shape, scale):
        return scale * jax.random.normal(next(ks), shape, jnp.float32)

    def unif(shape, lo, hi):
        return jax.random.uniform(next(ks), shape, jnp.float32, lo, hi)

    D = D_MODEL
    dt0 = jnp.exp(unif((N_A, SSM_HEADS), math.log(1e-3), math.log(1e-1)))
    s_lam = unif((N_B, LRU_WIDTH), 0.9, 0.999) ** (1.0 / LRU_C)
    return {
        'x_prompt': nrm((BATCH, SEQ, D), 1.0),
        'x_sample': nrm((DEC_BATCH, DEC_SEQ, D), 1.0),
        'state_ssm_conv': nrm((N_A, DEC_BATCH, SSM_CONV - 1, SSM_CONV_DIM), 1.0),
        'state_ssm': nrm((N_A, DEC_BATCH, SSM_HEADS, SSM_HEAD_DIM, SSM_STATE), 0.1),
        'state_lru_conv': nrm((N_B, DEC_BATCH, LRU_CONV - 1, LRU_WIDTH), 1.0),
        'state_lru': nrm((N_B, DEC_BATCH, LRU_WIDTH), 0.5),
        'state_sconv': nrm((N_C, DEC_BATCH, SC_WIDTH - 1, D), 1.0),
        'norm_mix': 1.0 + nrm((DEPTH, D), 0.01),
        'norm_ffn': 1.0 + nrm((DEPTH, D), 0.01),
        'norm_final': 1.0 + nrm((D,), 0.01),
        'w_ssm_in': nrm((N_A, D, SSM_IN_DIM), D ** -0.5),
        'ssm_conv_w': nrm((N_A, SSM_CONV, SSM_CONV_DIM), SSM_CONV ** -0.5),
        'ssm_conv_b': nrm((N_A, SSM_CONV_DIM), 0.01),
        'ssm_dt_bias': dt0 + jnp.log(-jnp.expm1(-dt0)),
        'ssm_A_log': jnp.log(unif((N_A, SSM_HEADS), 1.0, 16.0)),
        'ssm_D': 1.0 + nrm((N_A, SSM_HEADS), 0.1),
        'ssm_norm_w': 1.0 + nrm((N_A, SSM_INNER), 0.01),
        'w_ssm_out': nrm((N_A, SSM_INNER, D), SSM_INNER ** -0.5),
        'w_lru_in': nrm((N_B, D, 2 * LRU_WIDTH), D ** -0.5),
        'lru_conv_w': nrm((N_B, LRU_CONV, LRU_WIDTH), LRU_CONV ** -0.5),
        'lru_conv_b': nrm((N_B, LRU_WIDTH), 0.01),
        'lru_wa': nrm((N_B, LRU_HEADS, LRU_BLOCK, LRU_BLOCK), LRU_BLOCK ** -0.5),
        'lru_ba': nrm((N_B, LRU_WIDTH), 0.01),
        'lru_wi': nrm((N_B, LRU_HEADS, LRU_BLOCK, LRU_BLOCK), LRU_BLOCK ** -0.5),
        'lru_bi': nrm((N_B, LRU_WIDTH), 0.01),
        'lru_lambda': jnp.log(s_lam) - jnp.log1p(-s_lam),
        'w_lru_out': nrm((N_B, LRU_WIDTH, D), LRU_WIDTH ** -0.5),
        'w_sc_in': nrm((N_C, D, 3 * D), D ** -0.5),
        'sc_conv_w': nrm((N_C, SC_WIDTH, D), SC_WIDTH ** -0.5),
        'w_sc_out': nrm((N_C, D, D), D ** -0.5),
        'w_route_group': nrm((DEPTH, D, MOE_GROUPS), D ** -0.5),
        'w_route_expert': nrm((DEPTH, D, N_EXPERTS), D ** -0.5),
        'w_exp_gate': nrm((DEPTH, N_EXPERTS, D, MOE_FF), D ** -0.5),
        'w_exp_up': nrm((DEPTH, N_EXPERTS, D, MOE_FF), D ** -0.5),
        'w_exp_down': nrm((DEPTH, N_EXPERTS, MOE_FF, D), MOE_FF ** -0.5),
    }


def reference(x_prompt, x_sample, state_ssm_conv, state_ssm, state_lru_conv, state_lru, state_sconv,
              norm_mix, norm_ffn, norm_final,
              w_ssm_in, ssm_conv_w, ssm_conv_b, ssm_dt_bias, ssm_A_log, ssm_D, ssm_norm_w, w_ssm_out,
              w_lru_in, lru_conv_w, lru_conv_b, lru_wa, lru_ba, lru_wi, lru_bi, lru_lambda, w_lru_out,
              w_sc_in, sc_conv_w, w_sc_out,
              w_route_group, w_route_expert, w_exp_gate, w_exp_up, w_exp_down):
    p = dict(norm_mix=norm_mix, norm_ffn=norm_ffn, norm_final=norm_final,
             w_ssm_in=w_ssm_in, ssm_conv_w=ssm_conv_w, ssm_conv_b=ssm_conv_b, ssm_dt_bias=ssm_dt_bias,
             ssm_A_log=ssm_A_log, ssm_D=ssm_D, ssm_norm_w=ssm_norm_w, w_ssm_out=w_ssm_out,
             w_lru_in=w_lru_in, lru_conv_w=lru_conv_w, lru_conv_b=lru_conv_b, lru_wa=lru_wa, lru_ba=lru_ba,
             lru_wi=lru_wi, lru_bi=lru_bi, lru_lambda=lru_lambda, w_lru_out=w_lru_out,
             w_sc_in=w_sc_in, sc_conv_w=sc_conv_w, w_sc_out=w_sc_out,
             w_route_group=w_route_group, w_route_expert=w_route_expert,
             w_exp_gate=w_exp_gate, w_exp_up=w_exp_up, w_exp_down=w_exp_down)
    bp = x_prompt.shape[0]
    dt = x_prompt.dtype
    z_ssm_conv = jnp.zeros((N_A, bp, SSM_CONV - 1, SSM_CONV_DIM), dt)
    z_ssm = jnp.zeros((N_A, bp, SSM_HEADS, SSM_HEAD_DIM, SSM_STATE), jnp.float32)
    z_lru_conv = jnp.zeros((N_B, bp, LRU_CONV - 1, LRU_WIDTH), dt)
    z_lru = jnp.zeros((N_B, bp, LRU_WIDTH), jnp.float32)
    z_sconv = jnp.zeros((N_C, bp, SC_WIDTH - 1, D_MODEL), dt)
    y_prompt, (p_ssm_conv, p_ssm, p_lru_conv, p_lru, p_sconv) = run_trunk(
        x_prompt, z_ssm_conv, z_ssm, z_lru_conv, z_lru, z_sconv, p)
    y_sample, (s_ssm_conv, s_ssm, s_lru_conv, s_lru, s_sconv) = run_trunk(
        x_sample, state_ssm_conv, state_ssm, state_lru_conv, state_lru, state_sconv, p)
    return (y_prompt, y_sample, p_ssm_conv, p_ssm, p_lru_conv, p_lru, p_sconv,
            s_ssm_conv, s_ssm, s_lru_conv, s_lru, s_sconv)
```

```python
import functools
import math

import jax
import jax.numpy as jnp
from jax import lax
from jax.experimental import pallas as pl
from jax.experimental.pallas import tpu as pltpu

F32 = jnp.float32
BF16 = jnp.bfloat16

D_MODEL = 2048
DEPTH = 4
N_MIXERS = 3
RMS_EPS = 1e-6
SSM_INNER = 4096
SSM_HEAD_DIM = 64
SSM_HEADS = 64
SSM_GROUPS = 8
SSM_HPG = 8
SSM_STATE = 128
SSM_CONV = 4
SSM_BC = SSM_GROUPS * SSM_STATE
SSM_CONV_DIM = SSM_INNER + 2 * SSM_BC
SSM_CHUNK = 128
LRU_WIDTH = 2048
LRU_HEADS = 16
LRU_BLOCK = 128
LRU_CONV = 4
LRU_C = 8.0
SC_WIDTH = 3
MOE_GROUPS = 8
MOE_PER_GROUP = 4
N_EXPERTS = 32
MOE_FF = 512

LANES = 128
SUBLANES = 8
SAMPLE_PAD_T = 16
VMEM_LIMIT = 56 * 1024 * 1024


def _cparams(*sem):
    return pltpu.CompilerParams(dimension_semantics=sem, vmem_limit_bytes=VMEM_LIMIT)


def _silu(x):
    return x * jax.nn.sigmoid(x)


def _softplus(x):
    return jnp.maximum(x, 0.0) + jnp.log1p(jnp.exp(-jnp.abs(x)))


def _dot(a, b):
    return jnp.dot(a, b, preferred_element_type=F32)


def _dot_nt(a, b):
    return lax.dot_general(a, b, (((1,), (1,)), ((), ())), preferred_element_type=F32)


def _split3(x):
    hi = x.astype(BF16)
    r = x - hi.astype(F32)
    mid = r.astype(BF16)
    lo = (r - mid.astype(F32)).astype(BF16)
    return hi, mid, lo


def _norm_kernel(x_ref, w_ref, o_ref):
    x = x_ref[...]
    ms = jnp.mean(x * x, axis=-1, keepdims=True)
    o_ref[...] = (x * lax.rsqrt(ms + RMS_EPS) * w_ref[...]).astype(o_ref.dtype)


def _rmsnorm(x, w, tm, out_dtype):
    n, d = x.shape
    return pl.pallas_call(
        _norm_kernel,
        grid=(n // tm,),
        in_specs=[pl.BlockSpec((tm, d), lambda i: (i, 0)), pl.BlockSpec((1, d), lambda i: (0, 0))],
        out_specs=pl.BlockSpec((tm, d), lambda i: (i, 0)),
        out_shape=jax.ShapeDtypeStruct((n, d), out_dtype),
        compiler_params=_cparams("parallel"),
        name="rmsnorm",
    )(x, w.reshape(1, d))


def _mm_kernel(a_ref, w_ref, o_ref):
    o_ref[...] = _dot(a_ref[...], w_ref[...].astype(BF16))


def _matmul(a, w_stack, layer, col_blk0, ncols, tn, tm):
    n, k = a.shape
    return pl.pallas_call(
        _mm_kernel,
        grid=(n // tm, ncols // tn),
        in_specs=[
            pl.BlockSpec((tm, k), lambda i, j: (i, 0)),
            pl.BlockSpec((None, k, tn), lambda i, j: (layer, 0, col_blk0 + j)),
        ],
        out_specs=pl.BlockSpec((tm, tn), lambda i, j: (i, j)),
        out_shape=jax.ShapeDtypeStruct((n, ncols), F32),
        compiler_params=_cparams("parallel", "arbitrary"),
        name="in_proj",
    )(a, w_stack)


def _outproj_kernel(ap_ref, as_ref, w_ref, x_ref, nw_ref, rh_ref, rl_ref, x1_ref, xn_ref, lg_ref, acc_ref,
                    *, n_prompt_tiles, nk):
    i = pl.program_id(0)
    k = pl.program_id(1)

    @pl.when(k == 0)
    def _():
        acc_ref[...] = x_ref[...]

    w = w_ref[...].astype(BF16)

    @pl.when(i < n_prompt_tiles)
    def _():
        acc_ref[...] += _dot(ap_ref[...], w)

    @pl.when(i >= n_prompt_tiles)
    def _():
        acc_ref[...] += _dot(as_ref[...], w)

    @pl.when(k == nk - 1)
    def _():
        x1 = acc_ref[...]
        x1_ref[...] = x1
        ms = jnp.mean(x1 * x1, axis=-1, keepdims=True)
        xn = x1 * lax.rsqrt(ms + RMS_EPS) * nw_ref[...]
        xn_ref[...] = xn
        hi = xn.astype(BF16)
        lo = (xn - hi.astype(F32)).astype(BF16)
        rh = rh_ref[...]
        lg_ref[...] = _dot(hi, rh) + (_dot(hi, rl_ref[...]) + _dot(lo, rh))


def _outproj(act_p, act_s, w_stack, layer, x, norm_w, r_hi, r_lo, tk):
    n, d = x.shape
    n_p, kdim = act_p.shape
    tm = act_s.shape[0]
    npt = n_p // tm
    nk = kdim // tk
    kern = functools.partial(_outproj_kernel, n_prompt_tiles=npt, nk=nk)
    return pl.pallas_call(
        kern,
        grid=(n // tm, nk),
        in_specs=[
            pl.BlockSpec((tm, tk), lambda i, k: (jnp.minimum(i, npt - 1), jnp.where(i < npt, k, nk - 1))),
            pl.BlockSpec((tm, tk), lambda i, k: (0, jnp.where(i >= npt, k, 0))),
            pl.BlockSpec((None, tk, d), lambda i, k: (layer, k, 0)),
            pl.BlockSpec((tm, d), lambda i, k: (i, 0)),
            pl.BlockSpec((1, d), lambda i, k: (0, 0)),
            pl.BlockSpec((d, LANES), lambda i, k: (0, 0)),
            pl.BlockSpec((d, LANES), lambda i, k: (0, 0)),
        ],
        out_specs=[
            pl.BlockSpec((tm, d), lambda i, k: (i, 0)),
            pl.BlockSpec((tm, d), lambda i, k: (i, 0)),
            pl.BlockSpec((tm, LANES), lambda i, k: (i, 0)),
        ],
        out_shape=[
            jax.ShapeDtypeStruct((n, d), F32),
            jax.ShapeDtypeStruct((n, d), F32),
            jax.ShapeDtypeStruct((n, LANES), F32),
        ],
        scratch_shapes=[pltpu.VMEM((tm, d), F32)],
        compiler_params=_cparams("parallel", "arbitrary"),
        name="out_proj",
    )(act_p, act_s, w_stack, x, norm_w.reshape(1, d), r_hi, r_lo)


def _causal_conv(xp_ref, x, w_ref, bias, carry0_ref, width, rows, first):
    @pl.when(first)
    def _():
        xp_ref[0:SUBLANES, :] = carry0_ref[...]

    xp_ref[SUBLANES:SUBLANES + rows, :] = x
    base = SUBLANES - (width - 1)
    y = xp_ref[base:base + rows, :] * w_ref[0:1, :]
    for k in range(1, width):
        y = y + xp_ref[base + k:base + k + rows, :] * w_ref[k:k + 1, :]
    if bias is not None:
        y = y + bias
    xp_ref[0:SUBLANES, :] = xp_ref[rows:rows + SUBLANES, :]
    return y


def _pad_rows(a, rows):
    if a.shape[0] == rows:
        return a
    return jnp.concatenate([a, jnp.zeros((rows - a.shape[0], a.shape[1]), a.dtype)], axis=0)


def _ssd_kernel(z_ref, xbc_ref, dt_ref, carry0_ref, h0_ref, cw_ref, cb_ref, dtb_ref, alog_ref, dexp_ref, nw_ref,
                act_ref, ht_ref, xp_ref, h_ref, *, rows, n_chunks, t_valid):
    c = pl.program_id(1)
    lp = SSM_CHUNK

    @pl.when(c == 0)
    def _():
        h_ref[...] = h0_ref[...]

    conv = _causal_conv(xp_ref, xbc_ref[...], cw_ref, cb_ref[...], carry0_ref, SSM_CONV, rows, c == 0)
    xbc = _silu(conv)
    xs = xbc[:, :SSM_INNER]
    bm = xbc[:, SSM_INNER:SSM_INNER + SSM_BC]
    cm = xbc[:, SSM_INNER + SSM_BC:]

    row_q = lax.broadcasted_iota(jnp.int32, (rows, LANES), 0)
    dt = _softplus(dt_ref[...] + dtb_ref[...])
    if t_valid < rows:
        dt = jnp.where(row_q < t_valid, dt, 0.0)
    a_neg = -jnp.exp(alog_ref[...])
    dt_p = _pad_rows(dt, lp)
    ad = dt_p * a_neg
    r_i = lax.broadcasted_iota(jnp.int32, (lp, lp), 0)
    c_i = lax.broadcasted_iota(jnp.int32, (lp, lp), 1)
    tri = jnp.where(c_i <= r_i, 1.0, 0.0).astype(BF16)
    a_hi, a_mid, a_lo = _split3(ad)
    cs = _dot(tri, a_hi) + (_dot(tri, a_mid) + _dot(tri, a_lo))
    cs_t = cs.T
    dt_t = dt_p.T
    cs_last = cs[lp - 1:lp, :]
    to_end = jnp.exp(cs_last - cs)
    w_t = (dt_p * to_end).T
    e_last_t = jnp.broadcast_to(jnp.exp(cs_last), (lp, LANES)).T
    cs_q = cs[:rows, :]
    lane_q = lax.broadcasted_iota(jnp.int32, (rows, LANES), 1)
    causal = lax.broadcasted_iota(jnp.int32, (rows, lp), 1) <= lax.broadcasted_iota(jnp.int32, (rows, lp), 0)
    lane_lo = lane_q < SSM_HEAD_DIM
    lane_lo_p = lax.broadcasted_iota(jnp.int32, (lp, LANES), 1) < SSM_HEAD_DIM

    y_groups = []
    for g in range(SSM_GROUPS):
        cg = cm[:, g * SSM_STATE:(g + 1) * SSM_STATE].astype(BF16)
        bg = _pad_rows(bm[:, g * SSM_STATE:(g + 1) * SSM_STATE], lp).astype(BF16)
        gw = SSM_HPG * SSM_HEAD_DIM
        hg = h_ref[g * gw:(g + 1) * gw, :]
        cb = _dot_nt(cg, bg)
        yoff = _dot_nt(cg, hg.astype(BF16))
        xs_g = _pad_rows(xs[:, g * gw:(g + 1) * gw], lp)
        y_pairs = []
        for j in range(SSM_HPG // 2):
            ha = g * SSM_HPG + 2 * j
            hb = ha + 1
            col_a = jnp.sum(jnp.where(lane_q == ha, cs_q, 0.0), axis=1, keepdims=True)
            col_b = jnp.sum(jnp.where(lane_q == hb, cs_q, 0.0), axis=1, keepdims=True)
            m_a = cb * jnp.exp(jnp.where(causal, col_a - cs_t[ha:ha + 1, :], -jnp.inf)) * dt_t[ha:ha + 1, :]
            m_b = cb * jnp.exp(jnp.where(causal, col_b - cs_t[hb:hb + 1, :], -jnp.inf)) * dt_t[hb:hb + 1, :]
            xs_pair = xs_g[:, j * LANES:(j + 1) * LANES]
            xs_a = jnp.where(lane_lo_p, xs_pair, 0.0).astype(BF16)
            xs_b = jnp.where(lane_lo_p, 0.0, xs_pair).astype(BF16)
            ydiag = _dot(m_a.astype(BF16), xs_a) + _dot(m_b.astype(BF16), xs_b)
            scale = jnp.where(lane_lo, jnp.exp(col_a), jnp.exp(col_b))
            y_pairs.append(ydiag + yoff[:, j * LANES:(j + 1) * LANES] * scale)
        y_groups.append(jnp.concatenate(y_pairs, axis=1))
        xs_t = xs_g.T
        w_rows = jnp.concatenate(
            [jnp.broadcast_to(w_t[g * SSM_HPG + j:g * SSM_HPG + j + 1, :], (SSM_HEAD_DIM, lp)) for j in range(SSM_HPG)],
            axis=0)
        d_rows = jnp.concatenate(
            [jnp.broadcast_to(e_last_t[g * SSM_HPG + j:g * SSM_HPG + j + 1, :], (SSM_HEAD_DIM, LANES))
             for j in range(SSM_HPG)], axis=0)
        st = _dot((xs_t * w_rows).astype(BF16), bg)
        h_ref[g * gw:(g + 1) * gw, :] = hg * d_rows + st

    y = jnp.concatenate(y_groups, axis=1)
    y = y + xs * dexp_ref[...]
    y = y * _silu(z_ref[...])
    gsz = SSM_INNER // SSM_GROUPS
    outs = []
    for g in range(SSM_GROUPS):
        yg = y[:, g * gsz:(g + 1) * gsz]
        ms = jnp.mean(yg * yg, axis=-1, keepdims=True)
        outs.append(yg * lax.rsqrt(ms + RMS_EPS))
    act_ref[...] = (jnp.concatenate(outs, axis=1) * nw_ref[...]).astype(act_ref.dtype)

    @pl.when(c == n_chunks - 1)
    def _():
        ht_ref[...] = h_ref[...]


def _ssd_core(z, xbc, dt_raw, carry0, h0, conv_w, conv_b, dt_bias, a_log, d_exp, norm_w, *, nseq, rows, n_chunks,
              t_valid, h0_seq0=0):
    kern = functools.partial(_ssd_kernel, rows=rows, n_chunks=n_chunks, t_valid=t_valid)
    rmap = lambda b, c: (b * n_chunks + c, 0)
    const = lambda b, c: (0, 0)
    return pl.pallas_call(
        kern,
        grid=(nseq, n_chunks),
        in_specs=[
            pl.BlockSpec((rows, SSM_INNER), rmap),
            pl.BlockSpec((rows, SSM_CONV_DIM), rmap),
            pl.BlockSpec((rows, LANES), rmap),
            pl.BlockSpec((None, SUBLANES, SSM_CONV_DIM), lambda b, c: (b, 0, 0)),
            pl.BlockSpec((None, SSM_INNER, SSM_STATE), lambda b, c: (h0_seq0 + b, 0, 0)),
            pl.BlockSpec((SSM_CONV, SSM_CONV_DIM), const),
            pl.BlockSpec((1, SSM_CONV_DIM), const),
            pl.BlockSpec((1, LANES), const),
            pl.BlockSpec((1, LANES), const),
            pl.BlockSpec((1, SSM_INNER), const),
            pl.BlockSpec((1, SSM_INNER), const),
        ],
        out_specs=[
            pl.BlockSpec((rows, SSM_INNER), rmap),
            pl.BlockSpec((None, SSM_INNER, SSM_STATE), lambda b, c: (b, 0, 0)),
        ],
        out_shape=[
            jax.ShapeDtypeStruct((nseq * n_chunks * rows, SSM_INNER), BF16),
            jax.ShapeDtypeStruct((nseq, SSM_INNER, SSM_STATE), F32),
        ],
        scratch_shapes=[
            pltpu.VMEM((rows + SUBLANES, SSM_CONV_DIM), F32),
            pltpu.VMEM((SSM_INNER, SSM_STATE), F32),
        ],
        compiler_params=_cparams("parallel", "arbitrary"),
        name="ssd_core",
    )(z, xbc, dt_raw, carry0, h0, conv_w, conv_b, dt_bias, a_log, d_exp, norm_w)


def _lru_kernel(gin_ref, xr_ref, carry0_ref, h0_ref, cw_ref, cb_ref, wa_ref, ba_ref, wi_ref, bi_ref, lam_ref,
                act_ref, ht_ref, xp_ref, hc_ref, *, rows, n_chunks, t_valid):
    c = pl.program_id(1)

    @pl.when(c == 0)
    def _():
        hc_ref[...] = h0_ref[...]

    xc = _causal_conv(xp_ref, xr_ref[...], cw_ref, cb_ref[...], carry0_ref, LRU_CONV, rows, c == 0)
    ra = []
    ia = []
    for h in range(LRU_HEADS):
        xh = xc[:, h * LRU_BLOCK:(h + 1) * LRU_BLOCK].astype(BF16)
        ra.append(_dot(xh, wa_ref[h].astype(BF16)))
        ia.append(_dot(xh, wi_ref[h].astype(BF16)))
    r = jax.nn.sigmoid(jnp.concatenate(ra, axis=1) + ba_ref[...])
    ig = jax.nn.sigmoid(jnp.concatenate(ia, axis=1) + bi_ref[...])
    log_a = -LRU_C * r * _softplus(-lam_ref[...])
    a = jnp.exp(log_a)
    th = jnp.tanh(log_a)
    mult = jnp.sqrt(-2.0 * th / (1.0 - th))
    u = mult * (ig * xc)
    if t_valid < rows:
        row = lax.broadcasted_iota(jnp.int32, (rows, LRU_WIDTH), 0)
        a = jnp.where(row < t_valid, a, 1.0)
        u = jnp.where(row < t_valid, u, 0.0)

    sub = lax.broadcasted_iota(jnp.int32, (SUBLANES, LRU_WIDTH), 0)
    h_prev = hc_ref[...]
    h_blocks = []
    for blk in range(rows // SUBLANES):
        ab = a[blk * SUBLANES:(blk + 1) * SUBLANES, :]
        ub = u[blk * SUBLANES:(blk + 1) * SUBLANES, :]
        for d in (1, 2, 4):
            a_sh = jnp.where(sub >= d, pltpu.roll(ab, d, 0), 1.0)
            u_sh = jnp.where(sub >= d, pltpu.roll(ub, d, 0), 0.0)
            ub = ub + ab * u_sh
            ab = ab * a_sh
        hb = ub + ab * h_prev
        h_blocks.append(hb)
        h_prev = hb[SUBLANES - 1:SUBLANES, :]
    hc_ref[...] = h_prev
    hseq = jnp.concatenate(h_blocks, axis=0)
    gx = gin_ref[...]
    gate = 0.5 * gx * (1.0 + jnp.tanh(math.sqrt(2.0 / math.pi) * (gx + 0.044715 * (gx * gx * gx))))
    act_ref[...] = (hseq * gate).astype(act_ref.dtype)

    @pl.when(c == n_chunks - 1)
    def _():
        ht_ref[...] = hc_ref[...]


def _lru_core(proj, carry0, h0, conv_w, conv_b, wa, ba, wi, bi, lam, *, nseq, rows, n_chunks, t_valid):
    kern = functools.partial(_lru_kernel, rows=rows, n_chunks=n_chunks, t_valid=t_valid)
    w = LRU_WIDTH
    const = lambda b, c: (0, 0)
    const3 = lambda b, c: (0, 0, 0)
    return pl.pallas_call(
        kern,
        grid=(nseq, n_chunks),
        in_specs=[
            pl.BlockSpec((rows, w), lambda b, c: (b * n_chunks + c, 0)),
            pl.BlockSpec((rows, w), lambda b, c: (b * n_chunks + c, 1)),
            pl.BlockSpec((None, SUBLANES, w), lambda b, c: (b, 0, 0)),
            pl.BlockSpec((None, 1, w), lambda b, c: (b, 0, 0)),
            pl.BlockSpec((LRU_CONV, w), const),
            pl.BlockSpec((1, w), const),
            pl.BlockSpec((LRU_HEADS, LRU_BLOCK, LRU_BLOCK), const3),
            pl.BlockSpec((1, w), const),
            pl.BlockSpec((LRU_HEADS, LRU_BLOCK, LRU_BLOCK), const3),
            pl.BlockSpec((1, w), const),
            pl.BlockSpec((1, w), const),
        ],
        out_specs=[
            pl.BlockSpec((rows, w), lambda b, c: (b * n_chunks + c, 0)),
            pl.BlockSpec((None, 1, w), lambda b, c: (b, 0, 0)),
        ],
        out_shape=[
            jax.ShapeDtypeStruct((nseq * n_chunks * rows, w), BF16),
            jax.ShapeDtypeStruct((nseq, 1, w), F32),
        ],
        scratch_shapes=[pltpu.VMEM((rows + SUBLANES, w), F32), pltpu.VMEM((1, w), F32)],
        compiler_params=_cparams("parallel", "arbitrary"),
        name="lru_core",
    )(proj, proj, carry0, h0, conv_w, conv_b, wa, ba, wi, bi, lam)


def _sc_kernel(bg_ref, cg_ref, hh_ref, carry0_ref, cw_ref, act_ref, vt_ref, xp_ref, *, rows, n_chunks, t_valid):
    c = pl.program_id(1)
    v = cg_ref[...] * hh_ref[...]
    u = _causal_conv(xp_ref, v, cw_ref, None, carry0_ref, SC_WIDTH, rows, c == 0)
    act_ref[...] = (bg_ref[...] * u).astype(act_ref.dtype)
    lo = max(t_valid - SUBLANES, 0)

    @pl.when(c == n_chunks - 1)
    def _():
        vt_ref[...] = v[lo:lo + SUBLANES, :]


def _sc_core(proj, carry0, conv_w, *, nseq, rows, n_chunks, t_valid):
    kern = functools.partial(_sc_kernel, rows=rows, n_chunks=n_chunks, t_valid=t_valid)
    d = D_MODEL
    return pl.pallas_call(
        kern,
        grid=(nseq, n_chunks),
        in_specs=[
            pl.BlockSpec((rows, d), lambda b, c: (b * n_chunks + c, 0)),
            pl.BlockSpec((rows, d), lambda b, c: (b * n_chunks + c, 1)),
            pl.BlockSpec((rows, d), lambda b, c: (b * n_chunks + c, 2)),
            pl.BlockSpec((None, SUBLANES, d), lambda b, c: (b, 0, 0)),
            pl.BlockSpec((SC_WIDTH, d), lambda b, c: (0, 0)),
        ],
        out_specs=[
            pl.BlockSpec((rows, d), lambda b, c: (b * n_chunks + c, 0)),
            pl.BlockSpec((None, SUBLANES, d), lambda b, c: (b, 0, 0)),
        ],
        out_shape=[
            jax.ShapeDtypeStruct((nseq * n_chunks * rows, d), BF16),
            jax.ShapeDtypeStruct((nseq, SUBLANES, d), F32),
        ],
        scratch_shapes=[pltpu.VMEM((rows + SUBLANES, d), F32)],
        compiler_params=_cparams("parallel", "arbitrary"),
        name="sconv_core",
    )(proj, proj, proj, carry0, conv_w)


def _route_kernel(lg_ref, info_ref, cnt_ref, carry_ref, *, tm, n_tiles):
    i = pl.program_id(0)

    @pl.when(i == 0)
    def _():
        carry_ref[...] = jnp.zeros_like(carry_ref)

    lg = lg_ref[...]
    lane = lax.broadcasted_iota(jnp.int32, (tm, LANES), 1).astype(F32)
    neg = -jnp.inf
    big = float(LANES)
    is_g = lane < MOE_GROUPS
    gl = jnp.where(is_g, lg, neg)
    gmax = jnp.max(gl, axis=1, keepdims=True)
    gidx = jnp.min(jnp.where(gl == gmax, lane, big), axis=1, keepdims=True)
    gsum = jnp.sum(jnp.where(is_g, jnp.exp(lg - gmax), 0.0), axis=1, keepdims=True)
    gw = 1.0 / gsum
    rel = lane - (MOE_GROUPS + MOE_PER_GROUP * gidx)
    in_grp = (rel >= 0.0) & (rel < MOE_PER_GROUP)
    el = jnp.where(in_grp, lg, neg)
    v1 = jnp.max(el, axis=1, keepdims=True)
    i1 = jnp.min(jnp.where(in_grp & (el == v1), lane, big), axis=1, keepdims=True)
    in_grp2 = in_grp & (lane != i1)
    el2 = jnp.where(in_grp2, lg, neg)
    v2 = jnp.max(el2, axis=1, keepdims=True)
    i2 = jnp.min(jnp.where(in_grp2 & (el2 == v2), lane, big), axis=1, keepdims=True)
    t = jnp.exp(v2 - v1)
    den = 1.0 + t
    w1 = (1.0 / den) * gw
    w2 = (t / den) * gw
    e1 = i1 - MOE_GROUPS
    e2 = i2 - MOE_GROUPS
    onehot = jnp.where(lane == e1, 1.0, jnp.where(lane == e2, 1.0, 0.0))
    r_i = lax.broadcasted_iota(jnp.int32, (tm, tm), 0)
    c_i = lax.broadcasted_iota(jnp.int32, (tm, tm), 1)
    tri = jnp.where(c_i < r_i, 1.0, 0.0).astype(BF16)
    before = _dot(tri, onehot.astype(BF16)) + carry_ref[0:1, :]
    r1 = jnp.sum(jnp.where(lane == e1, before, 0.0), axis=1, keepdims=True)
    r2 = jnp.sum(jnp.where(lane == e2, before, 0.0), axis=1, keepdims=True)
    carry_ref[0:1, :] = carry_ref[0:1, :] + jnp.sum(onehot, axis=0, keepdims=True)
    info = jnp.where(lane == 0, w1, jnp.where(lane == 1, w2, jnp.where(lane == 2, e1, jnp.where(
        lane == 3, e2, jnp.where(lane == 4, r1, jnp.where(lane == 5, r2, 0.0))))))
    info_ref[...] = info

    @pl.when(i == n_tiles - 1)
    def _():
        cnt_ref[...] = carry_ref[...]


def _route(logits, tm):
    n = logits.shape[0]
    n_tiles = n // tm
    kern = functools.partial(_route_kernel, tm=tm, n_tiles=n_tiles)
    return pl.pallas_call(
        kern,
        grid=(n_tiles,),
        in_specs=[pl.BlockSpec((tm, LANES), lambda i: (i, 0))],
        out_specs=[pl.BlockSpec((tm, LANES), lambda i: (i, 0)), pl.BlockSpec((SUBLANES, LANES), lambda i: (0, 0))],
        out_shape=[jax.ShapeDtypeStruct((n, LANES), F32), jax.ShapeDtypeStruct((SUBLANES, LANES), F32)],
        scratch_shapes=[pltpu.VMEM((SUBLANES, LANES), F32)],
        compiler_params=_cparams("arbitrary"),
        name="route",
    )(logits)


def _ffn_kernel(vt_ref, ve_ref, vv_ref, off_ref, x_ref, wr_ref, wg_ref, wu_ref, wd_ref, o_ref, *, tm):
    v = pl.program_id(0)
    t = vt_ref[v]
    e = ve_ref[v]
    first = jnp.logical_or(v == 0, vt_ref[jnp.maximum(v - 1, 0)] != t)

    @pl.when(first)
    def _():
        o_ref[...] = jnp.zeros_like(o_ref)

    @pl.when(vv_ref[v] == 1)
    def _():
        row = t * tm + lax.broadcasted_iota(jnp.int32, (tm, MOE_FF), 0)
        mine = (row >= off_ref[e]) & (row < off_ref[e + 1])
        x = x_ref[...].astype(BF16)
        hg = _dot(x, wg_ref[...].astype(BF16))
        hu = _dot(x, wu_ref[...].astype(BF16))
        wr = wr_ref[...]
        hid = _silu(hg) * hu * jnp.concatenate([wr] * (MOE_FF // LANES), axis=1)
        hid = jnp.where(mine, hid, 0.0)
        o_ref[...] += _dot(hid.astype(BF16), wd_ref[...].astype(BF16))


def _ffn(xg, wrow, w_gate, w_up, w_down, layer, visit_tile, visit_exp, visit_valid, off, tm):
    rows, d = xg.shape
    n_visits = visit_tile.shape[0]
    kern = functools.partial(_ffn_kernel, tm=tm)
    grid_spec = pltpu.PrefetchScalarGridSpec(
        num_scalar_prefetch=4,
        grid=(n_visits,),
        in_specs=[
            pl.BlockSpec((tm, d), lambda v, vt, ve, vv, off: (vt[v], 0)),
            pl.BlockSpec((tm, LANES), lambda v, vt, ve, vv, off: (vt[v], 0)),
            pl.BlockSpec((None, None, d, MOE_FF), lambda v, vt, ve, vv, off: (layer, ve[v], 0, 0)),
            pl.BlockSpec((None, None, d, MOE_FF), lambda v, vt, ve, vv, off: (layer, ve[v], 0, 0)),
            pl.BlockSpec((None, None, MOE_FF, d), lambda v, vt, ve, vv, off: (layer, ve[v], 0, 0)),
        ],
        out_specs=pl.BlockSpec((tm, d), lambda v, vt, ve, vv, off: (vt[v], 0)),
    )
    return pl.pallas_call(
        kern,
        grid_spec=grid_spec,
        out_shape=jax.ShapeDtypeStruct((rows, d), F32),
        compiler_params=_cparams("arbitrary"),
        name="moe_ffn",
    )(visit_tile, visit_exp, visit_valid, off, xg, wrow, w_gate, w_up, w_down)


def _visit_plan(counts, n_tiles, tm):
    off = jnp.concatenate([jnp.zeros((1,), jnp.int32), jnp.cumsum(counts)])
    n_visits = n_tiles + N_EXPERTS - 1
    first_tile = off[:-1] // tm
    last_tile = jnp.where(counts > 0, (off[1:] - 1) // tm, first_tile)
    n_e = jnp.where(counts > 0, last_tile - first_tile + 1, 0)
    v_off = jnp.concatenate([jnp.zeros((1,), jnp.int32), jnp.cumsum(n_e)])
    total = v_off[-1]
    vid = jnp.arange(n_visits, dtype=jnp.int32)
    exp_of = jnp.sum((vid[:, None] >= v_off[None, 1:]).astype(jnp.int32), axis=1)
    exp_of = jnp.minimum(exp_of, N_EXPERTS - 1)
    tile_of = first_tile[exp_of] + (vid - v_off[exp_of])
    valid = vid < total
    last_e = jnp.sum((total - 1 >= v_off[1:]).astype(jnp.int32))
    last_e = jnp.minimum(last_e, N_EXPERTS - 1)
    exp_of = jnp.where(valid, exp_of, last_e)
    tile_of = jnp.where(valid, tile_of, n_tiles - 1)
    return tile_of.astype(jnp.int32), exp_of.astype(jnp.int32), valid.astype(jnp.int32), off.astype(jnp.int32)


def _combine_kernel(x_ref, y1_ref, y2_ref, w_ref, x2_ref, xn_ref):
    x2 = x_ref[...] + (y1_ref[...] + y2_ref[...])
    x2_ref[...] = x2
    ms = jnp.mean(x2 * x2, axis=-1, keepdims=True)
    xn_ref[...] = (x2 * lax.rsqrt(ms + RMS_EPS) * w_ref[...]).astype(xn_ref.dtype)


def _combine(x, y1, y2, w, tm, out_dtype):
    n, d = x.shape
    spec = pl.BlockSpec((tm, d), lambda i: (i, 0))
    return pl.pallas_call(
        _combine_kernel,
        grid=(n // tm,),
        in_specs=[spec, spec, spec, pl.BlockSpec((1, d), lambda i: (0, 0))],
        out_specs=[spec, spec],
        out_shape=[jax.ShapeDtypeStruct((n, d), F32), jax.ShapeDtypeStruct((n, d), out_dtype)],
        compiler_params=_cparams("parallel"),
        name="moe_combine",
    )(x, y1, y2, w.reshape(1, d))


def _pad_lanes(v, width=LANES):
    v = v.reshape(1, -1)
    return jnp.pad(v, ((0, 0), (0, width - v.shape[1])))


def _sample_rows(m, n_p, b_s, t_s):
    c = m.shape[1]
    s = m[n_p:].reshape(b_s, t_s, c)
    s = jnp.pad(s, ((0, 0), (0, SAMPLE_PAD_T - t_s), (0, 0)))
    return s.reshape(b_s * SAMPLE_PAD_T, c)


def _sample_act(act, b_s, t_s):
    c = act.shape[1]
    return act.reshape(b_s, SAMPLE_PAD_T, c)[:, :t_s].reshape(b_s * t_s, c)


def _carry_rows(state, width):
    return jnp.pad(state, ((0, 0), (SUBLANES - (width - 1), 0), (0, 0)))


def _moe(x1, xn, logits, layer, w_gate, w_up, w_down, next_norm_w, out_dtype, route_tm, ffn_tm, row_tm):
    n = x1.shape[0]
    info, cnt = _route(logits, route_tm)
    counts = cnt[0, :N_EXPERTS].astype(jnp.int32)
    w12 = info[:, 0:2]
    e12 = info[:, 2:4].astype(jnp.int32)
    r12 = info[:, 4:6].astype(jnp.int32)
    tile_of, exp_of, valid, off = _visit_plan(counts, (2 * n) // ffn_tm, ffn_tm)
    dest = off[e12] + r12
    slot_tok = jnp.zeros((2 * n,), jnp.int32).at[dest.reshape(-1)].set(
        jnp.repeat(jnp.arange(n, dtype=jnp.int32), 2))
    xg = jnp.take(xn, slot_tok, axis=0)
    wg = jnp.zeros((2 * n,), F32).at[dest.reshape(-1)].set(w12.reshape(-1))
    wrow = jnp.broadcast_to(wg[:, None], (2 * n, LANES))
    yg = _ffn(xg, wrow, w_gate, w_up, w_down, layer, tile_of, exp_of, valid, off, ffn_tm)
    y1 = jnp.take(yg, dest[:, 0], axis=0)
    y2 = jnp.take(yg, dest[:, 1], axis=0)
    return _combine(x1, y1, y2, next_norm_w, row_tm, out_dtype)


def kernel(x_prompt, x_sample, state_ssm_conv, state_ssm, state_lru_conv, state_lru, state_sconv, norm_mix, norm_ffn, norm_final, w_ssm_in, ssm_conv_w, ssm_conv_b, ssm_dt_bias, ssm_A_log, ssm_D, ssm_norm_w, w_ssm_out, w_lru_in, lru_conv_w, lru_conv_b, lru_wa, lru_ba, lru_wi, lru_bi, lru_lambda, w_lru_out, w_sc_in, sc_conv_w, w_sc_out, w_route_group, w_route_expert, w_exp_gate, w_exp_up, w_exp_down):
    b_p, t_p, d = x_prompt.shape
    b_s, t_s, _ = x_sample.shape
    n_p = b_p * t_p
    n_s = b_s * t_s
    n = n_p + n_s
    row_tm = n_s
    mm_tm = n // 4
    n_chunks_p = t_p // SSM_CHUNK

    x = jnp.concatenate([x_prompt.reshape(n_p, d), x_sample.reshape(n_s, d)], axis=0)
    xn = _rmsnorm(x, norm_mix[0], row_tm, BF16)

    p_ssm_conv, p_ssm, p_lru_conv, p_lru, p_sconv = [], [], [], [], []
    s_ssm_conv, s_ssm, s_lru_conv, s_lru, s_sconv = [], [], [], [], []

    for i in range(DEPTH):
        kind = i % N_MIXERS
        s = i // N_MIXERS
        w_r = jnp.pad(jnp.concatenate([w_route_group[i], w_route_expert[i]], axis=1),
                      ((0, 0), (0, LANES - MOE_GROUPS - N_EXPERTS)))
        r_hi = w_r.astype(BF16)
        r_lo = (w_r - r_hi.astype(F32)).astype(BF16)
        if kind == 0:
            z = _matmul(xn, w_ssm_in, s, 0, SSM_INNER, 512, mm_tm)
            xbc = _matmul(xn, w_ssm_in, s, SSM_INNER // 512, SSM_CONV_DIM, 512, mm_tm)
            w_dt = jnp.pad(w_ssm_in[s, :, SSM_INNER + SSM_CONV_DIM:], ((0, 0), (0, LANES - SSM_HEADS)))[None]
            dt_raw = _matmul(xn, w_dt, 0, 0, LANES, LANES, mm_tm)
            params = (ssm_conv_w[s], ssm_conv_b[s].reshape(1, -1), _pad_lanes(ssm_dt_bias[s]),
                      _pad_lanes(ssm_A_log[s]), jnp.repeat(ssm_D[s], SSM_HEAD_DIM).reshape(1, -1),
                      ssm_norm_w[s].reshape(1, -1))
            act_p, h_p = _ssd_core(
                z, xbc, dt_raw, jnp.zeros((b_p, SUBLANES, SSM_CONV_DIM), F32),
                jnp.zeros((b_p, SSM_INNER, SSM_STATE), F32), *params,
                nseq=b_p, rows=SSM_CHUNK, n_chunks=n_chunks_p, t_valid=SSM_CHUNK)
            act_s, h_s = _ssd_core(
                _sample_rows(z, n_p, b_s, t_s), _sample_rows(xbc, n_p, b_s, t_s), _sample_rows(dt_raw, n_p, b_s, t_s),
                _carry_rows(state_ssm_conv[s], SSM_CONV), state_ssm.reshape(-1, SSM_INNER, SSM_STATE), *params,
                nseq=b_s, rows=SAMPLE_PAD_T, n_chunks=1, t_valid=t_s, h0_seq0=s * b_s)
            act_s = _sample_act(act_s, b_s, t_s)
            p_ssm_conv.append(xbc[:n_p].reshape(b_p, t_p, -1)[:, t_p - (SSM_CONV - 1):])
            s_ssm_conv.append(jnp.concatenate(
                [state_ssm_conv[s], xbc[n_p:].reshape(b_s, t_s, -1)], axis=1)[:, t_s:])
            p_ssm.append(h_p.reshape(b_p, SSM_HEADS, SSM_HEAD_DIM, SSM_STATE))
            s_ssm.append(h_s.reshape(b_s, SSM_HEADS, SSM_HEAD_DIM, SSM_STATE))
            w_out = w_ssm_out
        elif kind == 1:
            proj = _matmul(xn, w_lru_in, s, 0, 2 * LRU_WIDTH, 512, mm_tm)
            params = (lru_conv_w[s], lru_conv_b[s].reshape(1, -1), lru_wa[s], lru_ba[s].reshape(1, -1), lru_wi[s],
                      lru_bi[s].reshape(1, -1), lru_lambda[s].reshape(1, -1))
            act_p, h_p = _lru_core(
                proj, jnp.zeros((b_p, SUBLANES, LRU_WIDTH), F32), jnp.zeros((b_p, 1, LRU_WIDTH), F32), *params,
                nseq=b_p, rows=SSM_CHUNK, n_chunks=n_chunks_p, t_valid=SSM_CHUNK)
            act_s, h_s = _lru_core(
                _sample_rows(proj, n_p, b_s, t_s), _carry_rows(state_lru_conv[s], LRU_CONV),
                state_lru[s].reshape(b_s, 1, LRU_WIDTH), *params,
                nseq=b_s, rows=SAMPLE_PAD_T, n_chunks=1, t_valid=t_s)
            act_s = _sample_act(act_s, b_s, t_s)
            xr = proj[:, LRU_WIDTH:]
            p_lru_conv.append(xr[:n_p].reshape(b_p, t_p, -1)[:, t_p - (LRU_CONV - 1):])
            s_lru_conv.append(jnp.concatenate(
                [state_lru_conv[s], xr[n_p:].reshape(b_s, t_s, -1)], axis=1)[:, t_s:])
            p_lru.append(h_p.reshape(b_p, LRU_WIDTH))
            s_lru.append(h_s.reshape(b_s, LRU_WIDTH))
            w_out = w_lru_out
        else:
            proj = _matmul(xn, w_sc_in, s, 0, 3 * d, 512, mm_tm)
            act_p, v_p = _sc_core(proj, jnp.zeros((b_p, SUBLANES, d), F32), sc_conv_w[s],
                                  nseq=b_p, rows=SSM_CHUNK, n_chunks=n_chunks_p, t_valid=SSM_CHUNK)
            act_s, v_s = _sc_core(_sample_rows(proj, n_p, b_s, t_s), _carry_rows(state_sconv[s], SC_WIDTH),
                                  sc_conv_w[s], nseq=b_s, rows=SAMPLE_PAD_T, n_chunks=1, t_valid=t_s)
            act_s = _sample_act(act_s, b_s, t_s)
            p_sconv.append(v_p[:, SUBLANES - (SC_WIDTH - 1):])
            s_sconv.append(jnp.concatenate([state_sconv[s], v_s[:, :t_s]], axis=1)[:, t_s:])
            w_out = w_sc_out

        x1, xn2, logits = _outproj(act_p, act_s, w_out, s, x, norm_ffn[i], r_hi, r_lo, 512)
        last = i == DEPTH - 1
        next_w = norm_final if last else norm_mix[i + 1]
        x, xn = _moe(x1, xn2, logits, i, w_exp_gate, w_exp_up, w_exp_down, next_w, F32 if last else BF16,
                     row_tm, 256, row_tm)

    y_prompt = xn[:n_p].reshape(b_p, t_p, d)
    y_sample = xn[n_p:].reshape(b_s, t_s, d)
    return (y_prompt, y_sample, jnp.stack(p_ssm_conv), jnp.stack(p_ssm), jnp.stack(p_lru_conv), jnp.stack(p_lru),
            jnp.stack(p_sconv), jnp.stack(s_ssm_conv), jnp.stack(s_ssm), jnp.stack(s_lru_conv), jnp.stack(s_lru),
            jnp.stack(s_sconv))
```

```python
import functools
import math

import jax
import jax.numpy as jnp
from jax import lax
from jax.experimental import pallas as pl
from jax.experimental.pallas import tpu as pltpu

F32 = jnp.float32
BF16 = jnp.bfloat16

D_MODEL = 2048
DEPTH = 4
N_MIXERS = 3
RMS_EPS = 1e-6
SSM_INNER = 4096
SSM_HEAD_DIM = 64
SSM_HEADS = 64
SSM_GROUPS = 8
SSM_HPG = 8
SSM_STATE = 128
SSM_CONV = 4
SSM_BC = SSM_GROUPS * SSM_STATE
SSM_CONV_DIM = SSM_INNER + 2 * SSM_BC
SSM_CHUNK = 128
LRU_WIDTH = 2048
LRU_HEADS = 16
LRU_BLOCK = 128
LRU_CONV = 4
LRU_C = 8.0
SC_WIDTH = 3
MOE_GROUPS = 8
MOE_PER_GROUP = 4
N_EXPERTS = 32
MOE_FF = 512

LANES = 128
SUBLANES = 8
SAMPLE_PAD_T = 16
VMEM_LIMIT = 56 * 1024 * 1024

ROW_TM = 512
PROJ_TN = 512
PROJ_TK = 512
ROUTE_TM = 512
DISPATCH_TM = 512
FFN_TM = 256
COMBINE_TM = 256


def _cparams(*sem):
    return pltpu.CompilerParams(dimension_semantics=sem, vmem_limit_bytes=VMEM_LIMIT)


def _silu(x):
    return x * jax.nn.sigmoid(x)


def _softplus(x):
    return jnp.maximum(x, 0.0) + jnp.log1p(jnp.exp(-jnp.abs(x)))


def _dot(a, b):
    return jnp.dot(a, b, preferred_element_type=F32)


def _dot_nt(a, b):
    return lax.dot_general(a, b, (((1,), (1,)), ((), ())), preferred_element_type=F32)


def _split3(x):
    hi = x.astype(BF16)
    r = x - hi.astype(F32)
    mid = r.astype(BF16)
    lo = (r - mid.astype(F32)).astype(BF16)
    return hi, mid, lo


def _norm_kernel(xp_ref, xs_ref, w_ref, o_ref, *, n_prompt_tiles):
    def emit(x):
        ms = jnp.mean(x * x, axis=-1, keepdims=True)
        o_ref[...] = (x * lax.rsqrt(ms + RMS_EPS) * w_ref[...]).astype(o_ref.dtype)

    @pl.when(pl.program_id(0) < n_prompt_tiles)
    def _():
        emit(xp_ref[...])

    @pl.when(pl.program_id(0) >= n_prompt_tiles)
    def _():
        emit(xs_ref[...])


def _rmsnorm(x_p, x_s, w, tm, out_dtype):
    n_p, d = x_p.shape
    n = n_p + x_s.shape[0]
    npt = n_p // tm
    return pl.pallas_call(
        functools.partial(_norm_kernel, n_prompt_tiles=npt),
        grid=(n // tm,),
        in_specs=[
            pl.BlockSpec((tm, d), lambda i: (jnp.minimum(i, npt - 1), 0)),
            pl.BlockSpec((tm, d), lambda i: (jnp.maximum(i - npt, 0), 0)),
            pl.BlockSpec((1, d), lambda i: (0, 0)),
        ],
        out_specs=pl.BlockSpec((tm, d), lambda i: (i, 0)),
        out_shape=jax.ShapeDtypeStruct((n, d), out_dtype),
        compiler_params=_cparams("arbitrary"),
        name="rmsnorm",
    )(x_p, x_s, w.reshape(1, d))


def _mm_kernel(a_ref, w_ref, o_ref, *, transposed):
    w = w_ref[...].astype(BF16)
    o_ref[...] = _dot_nt(a_ref[...], w) if transposed else _dot(a_ref[...], w)


def _matmul(a, w_stack, layer, col_blk0, ncols, tn, tm, transposed=False):
    n, k = a.shape
    if transposed:
        w_spec = pl.BlockSpec((None, tn, k), lambda i, j: (layer, col_blk0 + j, 0))
    else:
        w_spec = pl.BlockSpec((None, k, tn), lambda i, j: (layer, 0, col_blk0 + j))
    return pl.pallas_call(
        functools.partial(_mm_kernel, transposed=transposed),
        grid=(n // tm, ncols // tn),
        in_specs=[pl.BlockSpec((tm, k), lambda i, j: (i, 0)), w_spec],
        out_specs=pl.BlockSpec((tm, tn), lambda i, j: (i, j)),
        out_shape=jax.ShapeDtypeStruct((n, ncols), F32),
        compiler_params=_cparams("parallel", "arbitrary"),
        name="in_proj",
    )(a, w_stack)


def _outproj_kernel(ap_ref, as_ref, w_ref, xp_ref, xs_ref, nw_ref, rh_ref, rl_ref, x1_ref, lg_ref, acc_ref,
                    *, n_prompt_tiles, nk, tk):
    i = pl.program_id(0)
    k = pl.program_id(1)
    w = w_ref[pl.ds(pl.multiple_of(k * tk, tk), tk), :]

    @pl.when(i < n_prompt_tiles)
    def _():
        @pl.when(k == 0)
        def _():
            acc_ref[...] = xp_ref[...]

        acc_ref[...] += _dot(ap_ref[...], w)

    @pl.when(i >= n_prompt_tiles)
    def _():
        @pl.when(k == 0)
        def _():
            acc_ref[...] = xs_ref[...]

        acc_ref[...] += _dot(as_ref[...], w)

    @pl.when(k == nk - 1)
    def _():
        x1 = acc_ref[...]
        x1_ref[...] = x1
        ms = jnp.mean(x1 * x1, axis=-1, keepdims=True)
        xn = x1 * lax.rsqrt(ms + RMS_EPS) * nw_ref[...]
        hi = xn.astype(BF16)
        lo = (xn - hi.astype(F32)).astype(BF16)
        rh = rh_ref[...]
        lg_ref[...] = _dot(hi, rh) + (_dot(hi, rl_ref[...]) + _dot(lo, rh))


def _outproj(act_p, act_s, w_bf16, x_p, x_s, norm_w, r_hi, r_lo, tk):
    n_p, kdim = act_p.shape
    d = x_p.shape[1]
    tm = act_s.shape[0]
    n = n_p + tm
    npt = n_p // tm
    nk = kdim // tk
    kern = functools.partial(_outproj_kernel, n_prompt_tiles=npt, nk=nk, tk=tk)
    return pl.pallas_call(
        kern,
        grid=(n // tm, nk),
        in_specs=[
            pl.BlockSpec((tm, tk), lambda i, k: (jnp.minimum(i, npt - 1), jnp.where(i < npt, k, nk - 1))),
            pl.BlockSpec((tm, tk), lambda i, k: (0, jnp.where(i >= npt, k, 0))),
            pl.BlockSpec((kdim, d), lambda i, k: (0, 0), pipeline_mode=pl.Buffered(1)),
            pl.BlockSpec((tm, d), lambda i, k: (jnp.minimum(i, npt - 1), 0)),
            pl.BlockSpec((tm, d), lambda i, k: (0, 0)),
            pl.BlockSpec((1, d), lambda i, k: (0, 0)),
            pl.BlockSpec((d, LANES), lambda i, k: (0, 0)),
            pl.BlockSpec((d, LANES), lambda i, k: (0, 0)),
        ],
        out_specs=[
            pl.BlockSpec((tm, d), lambda i, k: (i, 0)),
            pl.BlockSpec((tm, LANES), lambda i, k: (i, 0)),
        ],
        out_shape=[
            jax.ShapeDtypeStruct((n, d), F32),
            jax.ShapeDtypeStruct((n, LANES), F32),
        ],
        scratch_shapes=[pltpu.VMEM((tm, d), F32)],
        compiler_params=_cparams("arbitrary", "arbitrary"),
        name="out_proj",
    )(act_p, act_s, w_bf16, x_p, x_s, norm_w.reshape(1, d), r_hi, r_lo)


def _causal_conv(xp_ref, x, w_ref, bias, carry0_ref, width, rows, first):
    @pl.when(first)
    def _():
        xp_ref[0:SUBLANES, :] = carry0_ref[...]

    xp_ref[SUBLANES:SUBLANES + rows, :] = x
    base = SUBLANES - (width - 1)
    y = xp_ref[base:base + rows, :] * w_ref[0:1, :]
    for k in range(1, width):
        y = y + xp_ref[base + k:base + k + rows, :] * w_ref[k:k + 1, :]
    if bias is not None:
        y = y + bias
    xp_ref[0:SUBLANES, :] = xp_ref[rows:rows + SUBLANES, :]
    return y


def _pad_rows(a, rows):
    if a.shape[0] == rows:
        return a
    return jnp.concatenate([a, jnp.zeros((rows - a.shape[0], a.shape[1]), a.dtype)], axis=0)


def _ssd_kernel(z_ref, xbc_ref, dt_ref, carry0_ref, h0_ref, cw_ref, cb_ref, dtb_ref, alog_ref, dexp_ref, nw_ref,
                ht_buf_ref, act_ref, ht_ref, xp_ref, h_ref, *, rows, n_chunks, t_valid):
    del ht_buf_ref
    c = pl.program_id(1)
    lp = SSM_CHUNK

    @pl.when(c == 0)
    def _():
        h_ref[...] = h0_ref[...]

    conv = _causal_conv(xp_ref, xbc_ref[...], cw_ref, cb_ref[...], carry0_ref, SSM_CONV, rows, c == 0)
    xbc = _silu(conv)
    xs = xbc[:, :SSM_INNER]
    bm = xbc[:, SSM_INNER:SSM_INNER + SSM_BC]
    cm = xbc[:, SSM_INNER + SSM_BC:]

    row_q = lax.broadcasted_iota(jnp.int32, (rows, LANES), 0)
    dt = _softplus(dt_ref[...] + dtb_ref[...])
    if t_valid < rows:
        dt = jnp.where(row_q < t_valid, dt, 0.0)
    a_neg = -jnp.exp(alog_ref[...])
    dt_p = _pad_rows(dt, lp)
    ad = dt_p * a_neg
    r_i = lax.broadcasted_iota(jnp.int32, (lp, lp), 0)
    c_i = lax.broadcasted_iota(jnp.int32, (lp, lp), 1)
    tri = jnp.where(c_i <= r_i, 1.0, 0.0).astype(BF16)
    a_hi, a_mid, a_lo = _split3(ad)
    cs = _dot(tri, a_hi) + (_dot(tri, a_mid) + _dot(tri, a_lo))
    cs_t = cs.T
    dt_t = dt_p.T
    cs_last = cs[lp - 1:lp, :]
    to_end = jnp.exp(cs_last - cs)
    w_t = (dt_p * to_end).T
    e_last_t = jnp.broadcast_to(jnp.exp(cs_last), (lp, LANES)).T
    cs_q = cs[:rows, :]
    lane_q = lax.broadcasted_iota(jnp.int32, (rows, LANES), 1)
    causal = lax.broadcasted_iota(jnp.int32, (rows, lp), 1) <= lax.broadcasted_iota(jnp.int32, (rows, lp), 0)
    lane_lo = lane_q < SSM_HEAD_DIM
    lane_lo_p = lax.broadcasted_iota(jnp.int32, (lp, LANES), 1) < SSM_HEAD_DIM

    y_groups = []
    for g in range(SSM_GROUPS):
        cg = cm[:, g * SSM_STATE:(g + 1) * SSM_STATE].astype(BF16)
        bg = _pad_rows(bm[:, g * SSM_STATE:(g + 1) * SSM_STATE], lp).astype(BF16)
        gw = SSM_HPG * SSM_HEAD_DIM
        hg = h_ref[g * gw:(g + 1) * gw, :]
        cb = _dot_nt(cg, bg)
        yoff = _dot_nt(cg, hg.astype(BF16))
        xs_g = _pad_rows(xs[:, g * gw:(g + 1) * gw], lp)
        y_pairs = []
        for j in range(SSM_HPG // 2):
            ha = g * SSM_HPG + 2 * j
            hb = ha + 1
            col_a = jnp.sum(jnp.where(lane_q == ha, cs_q, 0.0), axis=1, keepdims=True)
            col_b = jnp.sum(jnp.where(lane_q == hb, cs_q, 0.0), axis=1, keepdims=True)
            m_a = cb * jnp.exp(jnp.where(causal, col_a - cs_t[ha:ha + 1, :], -jnp.inf)) * dt_t[ha:ha + 1, :]
            m_b = cb * jnp.exp(jnp.where(causal, col_b - cs_t[hb:hb + 1, :], -jnp.inf)) * dt_t[hb:hb + 1, :]
            xs_pair = xs_g[:, j * LANES:(j + 1) * LANES]
            xs_a = jnp.where(lane_lo_p, xs_pair, 0.0).astype(BF16)
            xs_b = jnp.where(lane_lo_p, 0.0, xs_pair).astype(BF16)
            ydiag = _dot(m_a.astype(BF16), xs_a) + _dot(m_b.astype(BF16), xs_b)
            scale = jnp.where(lane_lo, jnp.exp(col_a), jnp.exp(col_b))
            y_pairs.append(ydiag + yoff[:, j * LANES:(j + 1) * LANES] * scale)
        y_groups.append(jnp.concatenate(y_pairs, axis=1))
        xs_t = xs_g.T
        w_rows = jnp.concatenate(
            [jnp.broadcast_to(w_t[g * SSM_HPG + j:g * SSM_HPG + j + 1, :], (SSM_HEAD_DIM, lp)) for j in range(SSM_HPG)],
            axis=0)
        d_rows = jnp.concatenate(
            [jnp.broadcast_to(e_last_t[g * SSM_HPG + j:g * SSM_HPG + j + 1, :], (SSM_HEAD_DIM, LANES))
             for j in range(SSM_HPG)], axis=0)
        st = _dot((xs_t * w_rows).astype(BF16), bg)
        h_ref[g * gw:(g + 1) * gw, :] = hg * d_rows + st

    y = jnp.concatenate(y_groups, axis=1)
    y = y + xs * dexp_ref[...]
    y = y * _silu(z_ref[...])
    gsz = SSM_INNER // SSM_GROUPS
    outs = []
    for g in range(SSM_GROUPS):
        yg = y[:, g * gsz:(g + 1) * gsz]
        ms = jnp.mean(yg * yg, axis=-1, keepdims=True)
        outs.append(yg * lax.rsqrt(ms + RMS_EPS))
    act_ref[...] = (jnp.concatenate(outs, axis=1) * nw_ref[...]).astype(act_ref.dtype)

    @pl.when(c == n_chunks - 1)
    def _():
        ht_ref[...] = h_ref[...]


def _ssd_core(z, xbc, dt_raw, carry0, h0, conv_w, conv_b, dt_bias, a_log, d_exp, norm_w, ht_buf, *, nseq, rows,
              n_chunks, t_valid, seq0=0):
    kern = functools.partial(_ssd_kernel, rows=rows, n_chunks=n_chunks, t_valid=t_valid)
    rmap = lambda b, c: (b * n_chunks + c, 0)
    const = lambda b, c: (0, 0)
    return pl.pallas_call(
        kern,
        grid=(nseq, n_chunks),
        in_specs=[
            pl.BlockSpec((rows, SSM_INNER), rmap),
            pl.BlockSpec((rows, SSM_CONV_DIM), rmap),
            pl.BlockSpec((rows, LANES), rmap),
            pl.BlockSpec((None, SUBLANES, SSM_CONV_DIM), lambda b, c: (b, 0, 0)),
            pl.BlockSpec((None, SSM_INNER, SSM_STATE), lambda b, c: (seq0 + b, 0, 0)),
            pl.BlockSpec((SSM_CONV, SSM_CONV_DIM), const),
            pl.BlockSpec((1, SSM_CONV_DIM), const),
            pl.BlockSpec((1, LANES), const),
            pl.BlockSpec((1, LANES), const),
            pl.BlockSpec((1, SSM_INNER), const),
            pl.BlockSpec((1, SSM_INNER), const),
            pl.BlockSpec(memory_space=pl.ANY),
        ],
        out_specs=[
            pl.BlockSpec((rows, SSM_INNER), rmap),
            pl.BlockSpec((None, SSM_INNER, SSM_STATE), lambda b, c: (seq0 + b, 0, 0)),
        ],
        out_shape=[
            jax.ShapeDtypeStruct((nseq * n_chunks * rows, SSM_INNER), BF16),
            jax.ShapeDtypeStruct(ht_buf.shape, F32),
        ],
        scratch_shapes=[
            pltpu.VMEM((rows + SUBLANES, SSM_CONV_DIM), F32),
            pltpu.VMEM((SSM_INNER, SSM_STATE), F32),
        ],
        input_output_aliases={11: 1},
        compiler_params=_cparams("arbitrary", "arbitrary"),
        name="ssd_core",
    )(z, xbc, dt_raw, carry0, h0, conv_w, conv_b, dt_bias, a_log, d_exp, norm_w, ht_buf)


def _lru_kernel(gin_ref, xr_ref, carry0_ref, h0_ref, cw_ref, cb_ref, wa_ref, ba_ref, wi_ref, bi_ref, lam_ref,
                act_ref, ht_ref, xp_ref, hc_ref, *, rows, n_chunks, t_valid):
    c = pl.program_id(1)

    @pl.when(c == 0)
    def _():
        hc_ref[...] = h0_ref[...]

    xc = _causal_conv(xp_ref, xr_ref[...], cw_ref, cb_ref[...], carry0_ref, LRU_CONV, rows, c == 0)
    ra = []
    ia = []
    for h in range(LRU_HEADS):
        xh = xc[:, h * LRU_BLOCK:(h + 1) * LRU_BLOCK].astype(BF16)
        ra.append(_dot(xh, wa_ref[h].astype(BF16)))
        ia.append(_dot(xh, wi_ref[h].astype(BF16)))
    r = jax.nn.sigmoid(jnp.concatenate(ra, axis=1) + ba_ref[...])
    ig = jax.nn.sigmoid(jnp.concatenate(ia, axis=1) + bi_ref[...])
    log_a = -LRU_C * r * _softplus(-lam_ref[...])
    a = jnp.exp(log_a)
    th = jnp.tanh(log_a)
    mult = jnp.sqrt(-2.0 * th / (1.0 - th))
    u = mult * (ig * xc)
    if t_valid < rows:
        row = lax.broadcasted_iota(jnp.int32, (rows, LRU_WIDTH), 0)
        a = jnp.where(row < t_valid, a, 1.0)
        u = jnp.where(row < t_valid, u, 0.0)

    sub = lax.broadcasted_iota(jnp.int32, (SUBLANES, LRU_WIDTH), 0)
    h_prev = hc_ref[...]
    h_blocks = []
    for blk in range(rows // SUBLANES):
        ab = a[blk * SUBLANES:(blk + 1) * SUBLANES, :]
        ub = u[blk * SUBLANES:(blk + 1) * SUBLANES, :]
        for d in (1, 2, 4):
            a_sh = jnp.where(sub >= d, pltpu.roll(ab, d, 0), 1.0)
            u_sh = jnp.where(sub >= d, pltpu.roll(ub, d, 0), 0.0)
            ub = ub + ab * u_sh
            ab = ab * a_sh
        hb = ub + ab * h_prev
        h_blocks.append(hb)
        h_prev = hb[SUBLANES - 1:SUBLANES, :]
    hc_ref[...] = h_prev
    hseq = jnp.concatenate(h_blocks, axis=0)
    gx = gin_ref[...]
    gate = 0.5 * gx * (1.0 + jnp.tanh(math.sqrt(2.0 / math.pi) * (gx + 0.044715 * (gx * gx * gx))))
    act_ref[...] = (hseq * gate).astype(act_ref.dtype)

    @pl.when(c == n_chunks - 1)
    def _():
        ht_ref[...] = hc_ref[...]


def _lru_core(proj, carry0, h0, conv_w, conv_b, wa, ba, wi, bi, lam, *, nseq, rows, n_chunks, t_valid):
    kern = functools.partial(_lru_kernel, rows=rows, n_chunks=n_chunks, t_valid=t_valid)
    w = LRU_WIDTH
    const = lambda b, c: (0, 0)
    const3 = lambda b, c: (0, 0, 0)
    return pl.pallas_call(
        kern,
        grid=(nseq, n_chunks),
        in_specs=[
            pl.BlockSpec((rows, w), lambda b, c: (b * n_chunks + c, 0)),
            pl.BlockSpec((rows, w), lambda b, c: (b * n_chunks + c, 1)),
            pl.BlockSpec((None, SUBLANES, w), lambda b, c: (b, 0, 0)),
            pl.BlockSpec((None, 1, w), lambda b, c: (b, 0, 0)),
            pl.BlockSpec((LRU_CONV, w), const),
            pl.BlockSpec((1, w), const),
            pl.BlockSpec((LRU_HEADS, LRU_BLOCK, LRU_BLOCK), const3),
            pl.BlockSpec((1, w), const),
            pl.BlockSpec((LRU_HEADS, LRU_BLOCK, LRU_BLOCK), const3),
            pl.BlockSpec((1, w), const),
            pl.BlockSpec((1, w), const),
        ],
        out_specs=[
            pl.BlockSpec((rows, w), lambda b, c: (b * n_chunks + c, 0)),
            pl.BlockSpec((None, 1, w), lambda b, c: (b, 0, 0)),
        ],
        out_shape=[
            jax.ShapeDtypeStruct((nseq * n_chunks * rows, w), BF16),
            jax.ShapeDtypeStruct((nseq, 1, w), F32),
        ],
        scratch_shapes=[pltpu.VMEM((rows + SUBLANES, w), F32), pltpu.VMEM((1, w), F32)],
        compiler_params=_cparams("parallel", "arbitrary"),
        name="lru_core",
    )(proj, proj, carry0, h0, conv_w, conv_b, wa, ba, wi, bi, lam)


def _sc_kernel(bg_ref, cg_ref, hh_ref, carry0_ref, cw_ref, act_ref, vt_ref, xp_ref, *, rows, n_chunks, t_valid):
    c = pl.program_id(1)
    v = cg_ref[...] * hh_ref[...]
    u = _causal_conv(xp_ref, v, cw_ref, None, carry0_ref, SC_WIDTH, rows, c == 0)
    act_ref[...] = (bg_ref[...] * u).astype(act_ref.dtype)
    lo = max(t_valid - SUBLANES, 0)

    @pl.when(c == n_chunks - 1)
    def _():
        vt_ref[...] = v[lo:lo + SUBLANES, :]


def _sc_core(proj, carry0, conv_w, *, nseq, rows, n_chunks, t_valid):
    kern = functools.partial(_sc_kernel, rows=rows, n_chunks=n_chunks, t_valid=t_valid)
    d = D_MODEL
    return pl.pallas_call(
        kern,
        grid=(nseq, n_chunks),
        in_specs=[
            pl.BlockSpec((rows, d), lambda b, c: (b * n_chunks + c, 0)),
            pl.BlockSpec((rows, d), lambda b, c: (b * n_chunks + c, 1)),
            pl.BlockSpec((rows, d), lambda b, c: (b * n_chunks + c, 2)),
            pl.BlockSpec((None, SUBLANES, d), lambda b, c: (b, 0, 0)),
            pl.BlockSpec((SC_WIDTH, d), lambda b, c: (0, 0)),
        ],
        out_specs=[
            pl.BlockSpec((rows, d), lambda b, c: (b * n_chunks + c, 0)),
            pl.BlockSpec((None, SUBLANES, d), lambda b, c: (b, 0, 0)),
        ],
        out_shape=[
            jax.ShapeDtypeStruct((nseq * n_chunks * rows, d), BF16),
            jax.ShapeDtypeStruct((nseq, SUBLANES, d), F32),
        ],
        scratch_shapes=[pltpu.VMEM((rows + SUBLANES, d), F32)],
        compiler_params=_cparams("parallel", "arbitrary"),
        name="sconv_core",
    )(proj, proj, proj, carry0, conv_w)


def _route_kernel(lg_ref, info_ref, w12_ref, cnt_ref, carry_ref, *, tm, n_tiles):
    i = pl.program_id(0)

    @pl.when(i == 0)
    def _():
        carry_ref[...] = jnp.zeros_like(carry_ref)

    lg = lg_ref[...]
    lane = lax.broadcasted_iota(jnp.int32, (tm, LANES), 1).astype(F32)
    neg = -jnp.inf
    big = float(LANES)
    is_g = lane < MOE_GROUPS
    gl = jnp.where(is_g, lg, neg)
    gmax = jnp.max(gl, axis=1, keepdims=True)
    gidx = jnp.min(jnp.where(gl == gmax, lane, big), axis=1, keepdims=True)
    gsum = jnp.sum(jnp.where(is_g, jnp.exp(lg - gmax), 0.0), axis=1, keepdims=True)
    gw = 1.0 / gsum
    rel = lane - (MOE_GROUPS + MOE_PER_GROUP * gidx)
    in_grp = (rel >= 0.0) & (rel < MOE_PER_GROUP)
    el = jnp.where(in_grp, lg, neg)
    v1 = jnp.max(el, axis=1, keepdims=True)
    i1 = jnp.min(jnp.where(in_grp & (el == v1), lane, big), axis=1, keepdims=True)
    in_grp2 = in_grp & (lane != i1)
    el2 = jnp.where(in_grp2, lg, neg)
    v2 = jnp.max(el2, axis=1, keepdims=True)
    i2 = jnp.min(jnp.where(in_grp2 & (el2 == v2), lane, big), axis=1, keepdims=True)
    t = jnp.exp(v2 - v1)
    den = 1.0 + t
    w1 = (1.0 / den) * gw
    w2 = (t / den) * gw
    e1 = i1 - MOE_GROUPS
    e2 = i2 - MOE_GROUPS
    onehot = jnp.where(lane == e1, 1.0, jnp.where(lane == e2, 1.0, 0.0))
    r_i = lax.broadcasted_iota(jnp.int32, (tm, tm), 0)
    c_i = lax.broadcasted_iota(jnp.int32, (tm, tm), 1)
    tri = jnp.where(c_i < r_i, 1.0, 0.0).astype(BF16)
    before = _dot(tri, onehot.astype(BF16)) + carry_ref[0:1, :]
    r1 = jnp.sum(jnp.where(lane == e1, before, 0.0), axis=1, keepdims=True)
    r2 = jnp.sum(jnp.where(lane == e2, before, 0.0), axis=1, keepdims=True)
    carry_ref[0:1, :] = carry_ref[0:1, :] + jnp.sum(onehot, axis=0, keepdims=True)
    info_ref[...] = jnp.where(lane == 0, e1, jnp.where(lane == 1, e2, jnp.where(lane == 2, r1, jnp.where(
        lane == 3, r2, 0.0))))
    w12_ref[...] = jnp.concatenate(
        [jnp.broadcast_to(w1, (tm, LANES)), jnp.broadcast_to(w2, (tm, LANES))], axis=1)

    @pl.when(i == n_tiles - 1)
    def _():
        cnt_ref[...] = carry_ref[...]


def _route(logits, tm):
    n = logits.shape[0]
    n_tiles = n // tm
    kern = functools.partial(_route_kernel, tm=tm, n_tiles=n_tiles)
    return pl.pallas_call(
        kern,
        grid=(n_tiles,),
        in_specs=[pl.BlockSpec((tm, LANES), lambda i: (i, 0))],
        out_specs=[
            pl.BlockSpec((tm, LANES), lambda i: (i, 0)),
            pl.BlockSpec((tm, 2 * LANES), lambda i: (i, 0)),
            pl.BlockSpec((SUBLANES, LANES), lambda i: (0, 0)),
        ],
        out_shape=[
            jax.ShapeDtypeStruct((n, LANES), F32),
            jax.ShapeDtypeStruct((n, 2 * LANES), F32),
            jax.ShapeDtypeStruct((SUBLANES, LANES), F32),
        ],
        scratch_shapes=[pltpu.VMEM((SUBLANES, LANES), F32)],
        compiler_params=_cparams("arbitrary"),
        name="route",
    )(logits)


def _dispatch_kernel(dest_ref, x_ref, nw_ref, xg_ref, xn_ref, sem, *, tm):
    base = pl.program_id(0) * tm
    x = x_ref[...]
    ms = jnp.mean(x * x, axis=-1, keepdims=True)
    xn_ref[...] = x * lax.rsqrt(ms + RMS_EPS) * nw_ref[...]

    def row_copy(r, d):
        return pltpu.make_async_copy(xn_ref.at[pl.ds(r, 1), :], xg_ref.at[pl.ds(d, 1), :], sem)

    def issue(r, carry):
        row_copy(r, dest_ref[2 * (base + r)]).start()
        row_copy(r, dest_ref[2 * (base + r) + 1]).start()
        return carry

    lax.fori_loop(0, tm, issue, 0, unroll=8)

    def drain(r, carry):
        row_copy(0, 0).wait()
        row_copy(0, 0).wait()
        return carry

    lax.fori_loop(0, tm, drain, 0, unroll=8)


def _dispatch(x1, norm_w, dest_flat, tm):
    n, d = x1.shape
    grid_spec = pltpu.PrefetchScalarGridSpec(
        num_scalar_prefetch=1,
        grid=(n // tm,),
        in_specs=[pl.BlockSpec((tm, d), lambda i, dest: (i, 0)), pl.BlockSpec((1, d), lambda i, dest: (0, 0))],
        out_specs=pl.BlockSpec(memory_space=pl.ANY),
        scratch_shapes=[pltpu.VMEM((tm, d), F32), pltpu.SemaphoreType.DMA(())],
    )
    return pl.pallas_call(
        functools.partial(_dispatch_kernel, tm=tm),
        grid_spec=grid_spec,
        out_shape=jax.ShapeDtypeStruct((2 * n, d), F32),
        compiler_params=_cparams("arbitrary"),
        name="moe_dispatch",
    )(dest_flat, x1, norm_w.reshape(1, d))


def _ffn_kernel(vt_ref, ve_ref, vv_ref, off_ref, x_ref, wg_ref, wu_ref, wd_ref, o_ref, *, tm):
    v = pl.program_id(0)
    t = vt_ref[v]
    e = ve_ref[v]
    first = jnp.logical_or(v == 0, vt_ref[jnp.maximum(v - 1, 0)] != t)

    @pl.when(first)
    def _():
        o_ref[...] = jnp.zeros_like(o_ref)

    @pl.when(vv_ref[v] == 1)
    def _():
        row = t * tm + lax.broadcasted_iota(jnp.int32, (tm, MOE_FF), 0)
        mine = (row >= off_ref[e]) & (row < off_ref[e + 1])
        x = x_ref[...].astype(BF16)
        hg = _dot(x, wg_ref[...].astype(BF16))
        hu = _dot(x, wu_ref[...].astype(BF16))
        hid = jnp.where(mine, _silu(hg) * hu, 0.0)
        o_ref[...] += _dot(hid.astype(BF16), wd_ref[...].astype(BF16))


def _ffn(xg, w_gate, w_up, w_down, layer, visit_tile, visit_exp, visit_valid, off, tm):
    rows, d = xg.shape
    n_visits = visit_tile.shape[0]
    kern = functools.partial(_ffn_kernel, tm=tm)
    grid_spec = pltpu.PrefetchScalarGridSpec(
        num_scalar_prefetch=4,
        grid=(n_visits,),
        in_specs=[
            pl.BlockSpec((tm, d), lambda v, vt, ve, vv, off: (vt[v], 0)),
            pl.BlockSpec((None, None, d, MOE_FF), lambda v, vt, ve, vv, off: (layer, ve[v], 0, 0)),
            pl.BlockSpec((None, None, d, MOE_FF), lambda v, vt, ve, vv, off: (layer, ve[v], 0, 0)),
            pl.BlockSpec((None, None, MOE_FF, d), lambda v, vt, ve, vv, off: (layer, ve[v], 0, 0)),
        ],
        out_specs=pl.BlockSpec((tm, d), lambda v, vt, ve, vv, off: (vt[v], 0)),
    )
    return pl.pallas_call(
        kern,
        grid_spec=grid_spec,
        out_shape=jax.ShapeDtypeStruct((rows, d), F32),
        compiler_params=_cparams("arbitrary"),
        name="moe_ffn",
    )(visit_tile, visit_exp, visit_valid, off, xg, w_gate, w_up, w_down)


def _visit_plan(counts, n_tiles, tm):
    off = jnp.concatenate([jnp.zeros((1,), jnp.int32), jnp.cumsum(counts)])
    n_visits = n_tiles + N_EXPERTS - 1
    first_tile = off[:-1] // tm
    last_tile = jnp.where(counts > 0, (off[1:] - 1) // tm, first_tile)
    n_e = jnp.where(counts > 0, last_tile - first_tile + 1, 0)
    v_off = jnp.concatenate([jnp.zeros((1,), jnp.int32), jnp.cumsum(n_e)])
    total = v_off[-1]
    vid = jnp.arange(n_visits, dtype=jnp.int32)
    exp_of = jnp.sum((vid[:, None] >= v_off[None, 1:]).astype(jnp.int32), axis=1)
    exp_of = jnp.minimum(exp_of, N_EXPERTS - 1)
    tile_of = first_tile[exp_of] + (vid - v_off[exp_of])
    valid = vid < total
    last_e = jnp.sum((total - 1 >= v_off[1:]).astype(jnp.int32))
    last_e = jnp.minimum(last_e, N_EXPERTS - 1)
    exp_of = jnp.where(valid, exp_of, last_e)
    tile_of = jnp.where(valid, tile_of, n_tiles - 1)
    return tile_of.astype(jnp.int32), exp_of.astype(jnp.int32), valid.astype(jnp.int32), off.astype(jnp.int32)


def _combine_kernel(dest_ref, x1_ref, w12_ref, nw_ref, yg_ref, *refs, tm, n_tiles, n_prompt_tiles, final):
    outs, (buf, sem) = refs[:-2], refs[-2:]
    i = pl.program_id(0)

    def row_copy(slot, k, r, d):
        return pltpu.make_async_copy(yg_ref.at[pl.ds(d, 1), :], buf.at[slot, k, pl.ds(r, 1), :], sem.at[slot])

    def issue(tile, slot):
        base = tile * tm

        def body(r, carry):
            row_copy(slot, 0, r, dest_ref[2 * (base + r)]).start()
            row_copy(slot, 1, r, dest_ref[2 * (base + r) + 1]).start()
            return carry

        lax.fori_loop(0, tm, body, 0, unroll=8)

    @pl.when(i == 0)
    def _():
        issue(0, 0)

    @pl.when(i + 1 < n_tiles)
    def _():
        issue(i + 1, lax.rem(i + 1, 2))

    slot = lax.rem(i, 2)

    def drain(r, carry):
        row_copy(slot, 0, 0, 0).wait()
        row_copy(slot, 1, 0, 0).wait()
        return carry

    lax.fori_loop(0, tm, drain, 0, unroll=8)

    w12 = w12_ref[...]
    reps = x1_ref.shape[1] // LANES
    w1 = jnp.concatenate([w12[:, :LANES]] * reps, axis=1)
    w2 = jnp.concatenate([w12[:, LANES:]] * reps, axis=1)
    x2 = x1_ref[...] + (w1 * buf[slot, 0] + w2 * buf[slot, 1])
    ms = jnp.mean(x2 * x2, axis=-1, keepdims=True)
    xn = x2 * lax.rsqrt(ms + RMS_EPS) * nw_ref[...]
    if final:
        split, split_val = outs, xn
    else:
        split, split_val = outs[:2], x2
        outs[2][...] = xn.astype(outs[2].dtype)

    @pl.when(i < n_prompt_tiles)
    def _():
        split[0][...] = split_val

    @pl.when(i >= n_prompt_tiles)
    def _():
        split[1][...] = split_val


def _combine(x1, w12, yg, dest_flat, norm_w, n_p, tm, final):
    n, d = x1.shape
    n_tiles = n // tm
    npt = n_p // tm
    spec_p = pl.BlockSpec((tm, d), lambda i, dest: (jnp.minimum(i, npt - 1), 0))
    spec_s = pl.BlockSpec((tm, d), lambda i, dest: (jnp.maximum(i - npt, 0), 0))
    out_specs = [spec_p, spec_s]
    out_shape = [jax.ShapeDtypeStruct((n_p, d), F32), jax.ShapeDtypeStruct((n - n_p, d), F32)]
    if not final:
        out_specs.append(pl.BlockSpec((tm, d), lambda i, dest: (i, 0)))
        out_shape.append(jax.ShapeDtypeStruct((n, d), BF16))
    grid_spec = pltpu.PrefetchScalarGridSpec(
        num_scalar_prefetch=1,
        grid=(n_tiles,),
        in_specs=[
            pl.BlockSpec((tm, d), lambda i, dest: (i, 0)),
            pl.BlockSpec((tm, 2 * LANES), lambda i, dest: (i, 0)),
            pl.BlockSpec((1, d), lambda i, dest: (0, 0)),
            pl.BlockSpec(memory_space=pl.ANY),
        ],
        out_specs=out_specs,
        scratch_shapes=[pltpu.VMEM((2, 2, tm, d), F32), pltpu.SemaphoreType.DMA((2,))],
    )
    return pl.pallas_call(
        functools.partial(_combine_kernel, tm=tm, n_tiles=n_tiles, n_prompt_tiles=npt, final=final),
        grid_spec=grid_spec,
        out_shape=out_shape,
        compiler_params=_cparams("arbitrary"),
        name="moe_combine",
    )(dest_flat, x1, w12, norm_w.reshape(1, d), yg)


def _pad_lanes(v, width=LANES):
    v = v.reshape(1, -1)
    return jnp.pad(v, ((0, 0), (0, width - v.shape[1])))


def _sample_rows(m, n_p, b_s, t_s):
    c = m.shape[1]
    s = m[n_p:].reshape(b_s, t_s, c)
    s = jnp.pad(s, ((0, 0), (0, SAMPLE_PAD_T - t_s), (0, 0)))
    return s.reshape(b_s * SAMPLE_PAD_T, c)


def _sample_act(act, b_s, t_s):
    c = act.shape[1]
    return act.reshape(b_s, SAMPLE_PAD_T, c)[:, :t_s].reshape(b_s * t_s, c)


def _carry_rows(state, width):
    return jnp.pad(state, ((0, 0), (SUBLANES - (width - 1), 0), (0, 0)))


def _moe(x1, ffn_norm_w, logits, layer, w_gate, w_up, w_down, next_norm_w, n_p, final):
    n = x1.shape[0]
    info, w12, cnt = _route(logits, ROUTE_TM)
    counts = cnt[0, :N_EXPERTS].astype(jnp.int32)
    e12 = info[:, 0:2].astype(jnp.int32)
    r12 = info[:, 2:4].astype(jnp.int32)
    tile_of, exp_of, valid, off = _visit_plan(counts, (2 * n) // FFN_TM, FFN_TM)
    dest = (off[e12] + r12).reshape(-1)
    xg = _dispatch(x1, ffn_norm_w, dest, DISPATCH_TM)
    yg = _ffn(xg, w_gate, w_up, w_down, layer, tile_of, exp_of, valid, off, FFN_TM)
    return _combine(x1, w12, yg, dest, next_norm_w, n_p, COMBINE_TM, final)


def kernel(x_prompt, x_sample, state_ssm_conv, state_ssm, state_lru_conv, state_lru, state_sconv, norm_mix, norm_ffn, norm_final, w_ssm_in, ssm_conv_w, ssm_conv_b, ssm_dt_bias, ssm_A_log, ssm_D, ssm_norm_w, w_ssm_out, w_lru_in, lru_conv_w, lru_conv_b, lru_wa, lru_ba, lru_wi, lru_bi, lru_lambda, w_lru_out, w_sc_in, sc_conv_w, w_sc_out, w_route_group, w_route_expert, w_exp_gate, w_exp_up, w_exp_down):
    b_p, t_p, d = x_prompt.shape
    b_s, t_s, _ = x_sample.shape
    n_p = b_p * t_p
    n_s = b_s * t_s
    n = n_p + n_s
    assert n_s == ROW_TM and n_p % ROW_TM == 0 and t_p % SSM_CHUNK == 0 and t_s <= SAMPLE_PAD_T
    mm_tm = n // 4
    n_chunks_p = t_p // SSM_CHUNK
    n_a = state_ssm.shape[0]

    x_p = x_prompt.reshape(n_p, d)
    x_s = x_sample.reshape(n_s, d)
    xn = _rmsnorm(x_p, x_s, norm_mix[0], ROW_TM, BF16)

    def last_rows_p(m, col0, ncol, k):
        return jnp.stack([m[(b + 1) * t_p - k:(b + 1) * t_p, col0:col0 + ncol] for b in range(b_p)])

    def last_rows_s(state, m, col0, ncol):
        new = m[n_p:, col0:col0 + ncol].reshape(b_s, t_s, ncol)
        return jnp.concatenate([state, new], axis=1)[:, t_s:]

    ssd_h_p = jnp.zeros((n_a * b_p, SSM_INNER, SSM_STATE), F32)
    ssd_h_s = jnp.zeros((n_a * b_s, SSM_INNER, SSM_STATE), F32)
    ssd_h0_p = jnp.zeros((n_a * b_p, SSM_INNER, SSM_STATE), F32)
    ssd_h0_s = state_ssm.reshape(n_a * b_s, SSM_INNER, SSM_STATE)
    w_ssm_in_t = jnp.swapaxes(w_ssm_in, 1, 2)

    p_ssm_conv, p_lru_conv, p_lru, p_sconv = [], [], [], []
    s_ssm_conv, s_lru_conv, s_lru, s_sconv = [], [], [], []

    for i in range(DEPTH):
        kind = i % N_MIXERS
        s = i // N_MIXERS
        w_r = jnp.pad(jnp.concatenate([w_route_group[i], w_route_expert[i]], axis=1),
                      ((0, 0), (0, LANES - MOE_GROUPS - N_EXPERTS)))
        r_hi = w_r.astype(BF16)
        r_lo = (w_r - r_hi.astype(F32)).astype(BF16)
        if kind == 0:
            z = _matmul(xn, w_ssm_in_t, s, 0, SSM_INNER, PROJ_TN, mm_tm, transposed=True)
            xbc = _matmul(xn, w_ssm_in_t, s, SSM_INNER // PROJ_TN, SSM_CONV_DIM, PROJ_TN, mm_tm, transposed=True)
            w_dt = jnp.pad(w_ssm_in_t[s, SSM_INNER + SSM_CONV_DIM:, :], ((0, LANES - SSM_HEADS), (0, 0)))[None]
            dt_raw = _matmul(xn, w_dt, 0, 0, LANES, LANES, mm_tm, transposed=True)
            params = (ssm_conv_w[s], ssm_conv_b[s].reshape(1, -1), _pad_lanes(ssm_dt_bias[s]),
                      _pad_lanes(ssm_A_log[s]), jnp.repeat(ssm_D[s], SSM_HEAD_DIM).reshape(1, -1),
                      ssm_norm_w[s].reshape(1, -1))
            act_p, ssd_h_p = _ssd_core(
                z, xbc, dt_raw, jnp.zeros((b_p, SUBLANES, SSM_CONV_DIM), F32), ssd_h0_p, *params, ssd_h_p,
                nseq=b_p, rows=SSM_CHUNK, n_chunks=n_chunks_p, t_valid=SSM_CHUNK, seq0=s * b_p)
            act_s, ssd_h_s = _ssd_core(
                _sample_rows(z, n_p, b_s, t_s), _sample_rows(xbc, n_p, b_s, t_s), _sample_rows(dt_raw, n_p, b_s, t_s),
                _carry_rows(state_ssm_conv[s], SSM_CONV), ssd_h0_s, *params, ssd_h_s,
                nseq=b_s, rows=SAMPLE_PAD_T, n_chunks=1, t_valid=t_s, seq0=s * b_s)
            act_s = _sample_act(act_s, b_s, t_s)
            p_ssm_conv.append(last_rows_p(xbc, 0, SSM_CONV_DIM, SSM_CONV - 1))
            s_ssm_conv.append(last_rows_s(state_ssm_conv[s], xbc, 0, SSM_CONV_DIM))
            w_out = w_ssm_out
        elif kind == 1:
            proj = _matmul(xn, w_lru_in, s, 0, 2 * LRU_WIDTH, PROJ_TN, mm_tm)
            params = (lru_conv_w[s], lru_conv_b[s].reshape(1, -1), lru_wa[s], lru_ba[s].reshape(1, -1), lru_wi[s],
                      lru_bi[s].reshape(1, -1), lru_lambda[s].reshape(1, -1))
            act_p, h_p = _lru_core(
                proj, jnp.zeros((b_p, SUBLANES, LRU_WIDTH), F32), jnp.zeros((b_p, 1, LRU_WIDTH), F32), *params,
                nseq=b_p, rows=SSM_CHUNK, n_chunks=n_chunks_p, t_valid=SSM_CHUNK)
            act_s, h_s = _lru_core(
                _sample_rows(proj, n_p, b_s, t_s), _carry_rows(state_lru_conv[s], LRU_CONV),
                state_lru[s].reshape(b_s, 1, LRU_WIDTH), *params,
                nseq=b_s, rows=SAMPLE_PAD_T, n_chunks=1, t_valid=t_s)
            act_s = _sample_act(act_s, b_s, t_s)
            p_lru_conv.append(last_rows_p(proj, LRU_WIDTH, LRU_WIDTH, LRU_CONV - 1))
            s_lru_conv.append(last_rows_s(state_lru_conv[s], proj, LRU_WIDTH, LRU_WIDTH))
            p_lru.append(h_p.reshape(b_p, LRU_WIDTH))
            s_lru.append(h_s.reshape(b_s, LRU_WIDTH))
            w_out = w_lru_out
        else:
            proj = _matmul(xn, w_sc_in, s, 0, 3 * d, PROJ_TN, mm_tm)
            act_p, v_p = _sc_core(proj, jnp.zeros((b_p, SUBLANES, d), F32), sc_conv_w[s],
                                  nseq=b_p, rows=SSM_CHUNK, n_chunks=n_chunks_p, t_valid=SSM_CHUNK)
            act_s, v_s = _sc_core(_sample_rows(proj, n_p, b_s, t_s), _carry_rows(state_sconv[s], SC_WIDTH),
                                  sc_conv_w[s], nseq=b_s, rows=SAMPLE_PAD_T, n_chunks=1, t_valid=t_s)
            act_s = _sample_act(act_s, b_s, t_s)
            p_sconv.append(v_p[:, SUBLANES - (SC_WIDTH - 1):])
            s_sconv.append(jnp.concatenate([state_sconv[s], v_s[:, :t_s]], axis=1)[:, t_s:])
            w_out = w_sc_out

        x1, logits = _outproj(act_p, act_s, w_out[s].astype(BF16), x_p, x_s, norm_ffn[i], r_hi, r_lo, PROJ_TK)
        moe_w = (w_exp_gate, w_exp_up, w_exp_down)
        if i < DEPTH - 1:
            x_p, x_s, xn = _moe(x1, norm_ffn[i], logits, i, *moe_w, norm_mix[i + 1], n_p, False)
        else:
            y_p, y_s = _moe(x1, norm_ffn[i], logits, i, *moe_w, norm_final, n_p, True)

    state_shape = (SSM_HEADS, SSM_HEAD_DIM, SSM_STATE)
    return (y_p.reshape(b_p, t_p, d), y_s.reshape(b_s, t_s, d),
            jnp.stack(p_ssm_conv), ssd_h_p.reshape((n_a, b_p) + state_shape), jnp.stack(p_lru_conv), jnp.stack(p_lru),
            jnp.stack(p_sconv),
            jnp.stack(s_ssm_conv), ssd_h_s.reshape((n_a, b_s) + state_shape), jnp.stack(s_lru_conv), jnp.stack(s_lru),
            jnp.stack(s_sconv))
```

```python
import functools
import math

import jax
import jax.numpy as jnp
from jax import lax
from jax.experimental import pallas as pl
from jax.experimental.pallas import tpu as pltpu

F32 = jnp.float32
BF16 = jnp.bfloat16

D_MODEL = 2048
DEPTH = 4
N_MIXERS = 3
RMS_EPS = 1e-6
SSM_INNER = 4096
SSM_HEAD_DIM = 64
SSM_HEADS = 64
SSM_GROUPS = 8
SSM_HPG = 8
SSM_STATE = 128
SSM_CONV = 4
SSM_BC = SSM_GROUPS * SSM_STATE
SSM_CONV_DIM = SSM_INNER + 2 * SSM_BC
SSM_CHUNK = 128
LRU_WIDTH = 2048
LRU_HEADS = 16
LRU_BLOCK = 128
LRU_CONV = 4
LRU_C = 8.0
SC_WIDTH = 3
MOE_GROUPS = 8
MOE_PER_GROUP = 4
N_EXPERTS = 32
MOE_FF = 512

LANES = 128
SUBLANES = 8
SAMPLE_PAD_T = 16
VMEM_LIMIT = 56 * 1024 * 1024

ROW_TM = 512
PROJ_TN = 512
PROJ_TK = 1024
ROUTE_TM = 512
DISPATCH_TM = 512
FFN_TM = 256
COMBINE_TM = 256


def _cparams(*sem):
    return pltpu.CompilerParams(dimension_semantics=sem, vmem_limit_bytes=VMEM_LIMIT)


def _silu(x):
    return x * jax.nn.sigmoid(x)


def _softplus(x):
    return jnp.maximum(x, 0.0) + jnp.log1p(jnp.exp(-jnp.abs(x)))


def _dot(a, b):
    return jnp.dot(a, b, preferred_element_type=F32)


def _dot_nt(a, b):
    return lax.dot_general(a, b, (((1,), (1,)), ((), ())), preferred_element_type=F32)


def _split3(x):
    hi = x.astype(BF16)
    r = x - hi.astype(F32)
    mid = r.astype(BF16)
    lo = (r - mid.astype(F32)).astype(BF16)
    return hi, mid, lo


def _norm_kernel(xp_ref, xs_ref, w_ref, o_ref, *, n_prompt_tiles):
    def emit(x):
        ms = jnp.mean(x * x, axis=-1, keepdims=True)
        o_ref[...] = (x * lax.rsqrt(ms + RMS_EPS) * w_ref[...]).astype(o_ref.dtype)

    @pl.when(pl.program_id(0) < n_prompt_tiles)
    def _():
        emit(xp_ref[...])

    @pl.when(pl.program_id(0) >= n_prompt_tiles)
    def _():
        emit(xs_ref[...])


def _rmsnorm(x_p, x_s, w, tm, out_dtype):
    n_p, d = x_p.shape
    n = n_p + x_s.shape[0]
    npt = n_p // tm
    return pl.pallas_call(
        functools.partial(_norm_kernel, n_prompt_tiles=npt),
        grid=(n // tm,),
        in_specs=[
            pl.BlockSpec((tm, d), lambda i: (jnp.minimum(i, npt - 1), 0)),
            pl.BlockSpec((tm, d), lambda i: (jnp.maximum(i - npt, 0), 0)),
            pl.BlockSpec((1, d), lambda i: (0, 0)),
        ],
        out_specs=pl.BlockSpec((tm, d), lambda i: (i, 0)),
        out_shape=jax.ShapeDtypeStruct((n, d), out_dtype),
        compiler_params=_cparams("arbitrary"),
        name="rmsnorm",
    )(x_p, x_s, w.reshape(1, d))


def _mm_kernel(a_ref, w_ref, o_ref, *, transposed):
    w = w_ref[...].astype(BF16)
    o_ref[...] = _dot_nt(a_ref[...], w) if transposed else _dot(a_ref[...], w)


def _matmul(a, w_stack, layer, col_blk0, ncols, tn, tm, transposed=False):
    n, k = a.shape
    if transposed:
        w_spec = pl.BlockSpec((None, tn, k), lambda i, j: (layer, col_blk0 + j, 0))
    else:
        w_spec = pl.BlockSpec((None, k, tn), lambda i, j: (layer, 0, col_blk0 + j))
    return pl.pallas_call(
        functools.partial(_mm_kernel, transposed=transposed),
        grid=(n // tm, ncols // tn),
        in_specs=[pl.BlockSpec((tm, k), lambda i, j: (i, 0)), w_spec],
        out_specs=pl.BlockSpec((tm, tn), lambda i, j: (i, j)),
        out_shape=jax.ShapeDtypeStruct((n, ncols), F32),
        compiler_params=_cparams("parallel", "arbitrary"),
        name="in_proj",
    )(a, w_stack)


def _outproj_kernel(ap_ref, as_ref, w_ref, xp_ref, xs_ref, nw_ref, rh_ref, rl_ref, x1_ref, lg_ref, acc_ref,
                    *, n_prompt_tiles, nk, tk):
    i = pl.program_id(0)
    k = pl.program_id(1)
    w = w_ref[pl.ds(pl.multiple_of(k * tk, tk), tk), :]

    @pl.when(i < n_prompt_tiles)
    def _():
        @pl.when(k == 0)
        def _():
            acc_ref[...] = xp_ref[...]

        acc_ref[...] += _dot(ap_ref[...], w)

    @pl.when(i >= n_prompt_tiles)
    def _():
        @pl.when(k == 0)
        def _():
            acc_ref[...] = xs_ref[...]

        acc_ref[...] += _dot(as_ref[...], w)

    @pl.when(k == nk - 1)
    def _():
        x1 = acc_ref[...]
        x1_ref[...] = x1
        ms = jnp.mean(x1 * x1, axis=-1, keepdims=True)
        xn = x1 * lax.rsqrt(ms + RMS_EPS) * nw_ref[...]
        hi = xn.astype(BF16)
        lo = (xn - hi.astype(F32)).astype(BF16)
        rh = rh_ref[...]
        lg_ref[...] = _dot(hi, rh) + (_dot(hi, rl_ref[...]) + _dot(lo, rh))


def _outproj(act_p, act_s, w_bf16, x_p, x_s, norm_w, r_hi, r_lo, tk):
    n_p, kdim = act_p.shape
    d = x_p.shape[1]
    tm = act_s.shape[0]
    n = n_p + tm
    npt = n_p // tm
    nk = kdim // tk
    kern = functools.partial(_outproj_kernel, n_prompt_tiles=npt, nk=nk, tk=tk)
    return pl.pallas_call(
        kern,
        grid=(n // tm, nk),
        in_specs=[
            pl.BlockSpec((tm, tk), lambda i, k: (jnp.minimum(i, npt - 1), jnp.where(i < npt, k, nk - 1))),
            pl.BlockSpec((tm, tk), lambda i, k: (0, jnp.where(i >= npt, k, 0))),
            pl.BlockSpec((kdim, d), lambda i, k: (0, 0), pipeline_mode=pl.Buffered(1)),
            pl.BlockSpec((tm, d), lambda i, k: (jnp.minimum(i, npt - 1), 0)),
            pl.BlockSpec((tm, d), lambda i, k: (0, 0), pipeline_mode=pl.Buffered(1)),
            pl.BlockSpec((1, d), lambda i, k: (0, 0)),
            pl.BlockSpec((d, LANES), lambda i, k: (0, 0)),
            pl.BlockSpec((d, LANES), lambda i, k: (0, 0)),
        ],
        out_specs=[
            pl.BlockSpec((tm, d), lambda i, k: (i, 0)),
            pl.BlockSpec((tm, LANES), lambda i, k: (i, 0)),
        ],
        out_shape=[
            jax.ShapeDtypeStruct((n, d), F32),
            jax.ShapeDtypeStruct((n, LANES), F32),
        ],
        scratch_shapes=[pltpu.VMEM((tm, d), F32)],
        compiler_params=_cparams("arbitrary", "arbitrary"),
        name="out_proj",
    )(act_p, act_s, w_bf16, x_p, x_s, norm_w.reshape(1, d), r_hi, r_lo)


def _causal_conv(xp_ref, x, w_ref, bias, carry0_ref, width, rows, first):
    @pl.when(first)
    def _():
        xp_ref[0:SUBLANES, :] = carry0_ref[...]

    xp_ref[SUBLANES:SUBLANES + rows, :] = x
    base = SUBLANES - (width - 1)
    y = xp_ref[base:base + rows, :] * w_ref[0:1, :]
    for k in range(1, width):
        y = y + xp_ref[base + k:base + k + rows, :] * w_ref[k:k + 1, :]
    if bias is not None:
        y = y + bias
    xp_ref[0:SUBLANES, :] = xp_ref[rows:rows + SUBLANES, :]
    return y


def _pad_rows(a, rows):
    if a.shape[0] == rows:
        return a
    return jnp.concatenate([a, jnp.zeros((rows - a.shape[0], a.shape[1]), a.dtype)], axis=0)


def _ssd_kernel(z_ref, xbc_ref, dt_ref, carry0_ref, h0_ref, cw_ref, cb_ref, dtb_ref, alog_ref, dexp_ref, nw_ref,
                ht_buf_ref, act_ref, ht_ref, xp_ref, h_ref, *, rows, n_chunks, t_valid):
    del ht_buf_ref
    c = pl.program_id(1)
    lp = SSM_CHUNK

    @pl.when(c == 0)
    def _():
        h_ref[...] = h0_ref[...]

    conv = _causal_conv(xp_ref, xbc_ref[...], cw_ref, cb_ref[...], carry0_ref, SSM_CONV, rows, c == 0)
    xbc = _silu(conv)
    xs = xbc[:, :SSM_INNER]
    bm = xbc[:, SSM_INNER:SSM_INNER + SSM_BC]
    cm = xbc[:, SSM_INNER + SSM_BC:]

    row_q = lax.broadcasted_iota(jnp.int32, (rows, LANES), 0)
    dt = _softplus(dt_ref[...] + dtb_ref[...])
    if t_valid < rows:
        dt = jnp.where(row_q < t_valid, dt, 0.0)
    a_neg = -jnp.exp(alog_ref[...])
    dt_p = _pad_rows(dt, lp)
    ad = dt_p * a_neg
    r_i = lax.broadcasted_iota(jnp.int32, (lp, lp), 0)
    c_i = lax.broadcasted_iota(jnp.int32, (lp, lp), 1)
    tri = jnp.where(c_i <= r_i, 1.0, 0.0).astype(BF16)
    a_hi, a_mid, a_lo = _split3(ad)
    cs = _dot(tri, a_hi) + (_dot(tri, a_mid) + _dot(tri, a_lo))
    cs_t = cs.T
    dt_t = dt_p.T
    cs_last = cs[lp - 1:lp, :]
    to_end = jnp.exp(cs_last - cs)
    w_t = (dt_p * to_end).T
    e_last_t = jnp.broadcast_to(jnp.exp(cs_last), (lp, LANES)).T
    cs_q = cs[:rows, :]
    lane_q = lax.broadcasted_iota(jnp.int32, (rows, LANES), 1)
    causal = lax.broadcasted_iota(jnp.int32, (rows, lp), 1) <= lax.broadcasted_iota(jnp.int32, (rows, lp), 0)
    lane_lo = lane_q < SSM_HEAD_DIM
    lane_lo_p = lax.broadcasted_iota(jnp.int32, (lp, LANES), 1) < SSM_HEAD_DIM

    y_groups = []
    for g in range(SSM_GROUPS):
        cg = cm[:, g * SSM_STATE:(g + 1) * SSM_STATE].astype(BF16)
        bg = _pad_rows(bm[:, g * SSM_STATE:(g + 1) * SSM_STATE], lp).astype(BF16)
        gw = SSM_HPG * SSM_HEAD_DIM
        hg = h_ref[g * gw:(g + 1) * gw, :]
        cb = _dot_nt(cg, bg)
        yoff = _dot_nt(cg, hg.astype(BF16))
        xs_g = _pad_rows(xs[:, g * gw:(g + 1) * gw], lp)
        y_pairs = []
        for j in range(SSM_HPG // 2):
            ha = g * SSM_HPG + 2 * j
            hb = ha + 1
            col_a = jnp.sum(jnp.where(lane_q == ha, cs_q, 0.0), axis=1, keepdims=True)
            col_b = jnp.sum(jnp.where(lane_q == hb, cs_q, 0.0), axis=1, keepdims=True)
            m_a = cb * jnp.exp(jnp.where(causal, col_a - cs_t[ha:ha + 1, :], -jnp.inf)) * dt_t[ha:ha + 1, :]
            m_b = cb * jnp.exp(jnp.where(causal, col_b - cs_t[hb:hb + 1, :], -jnp.inf)) * dt_t[hb:hb + 1, :]
            xs_pair = xs_g[:, j * LANES:(j + 1) * LANES]
            xs_a = jnp.where(lane_lo_p, xs_pair, 0.0).astype(BF16)
            xs_b = jnp.where(lane_lo_p, 0.0, xs_pair).astype(BF16)
            ydiag = _dot(m_a.astype(BF16), xs_a) + _dot(m_b.astype(BF16), xs_b)
            scale = jnp.where(lane_lo, jnp.exp(col_a), jnp.exp(col_b))
            y_pairs.append(ydiag + yoff[:, j * LANES:(j + 1) * LANES] * scale)
        y_groups.append(jnp.concatenate(y_pairs, axis=1))
        xs_t = xs_g.T
        w_rows = jnp.concatenate(
            [jnp.broadcast_to(w_t[g * SSM_HPG + j:g * SSM_HPG + j + 1, :], (SSM_HEAD_DIM, lp)) for j in range(SSM_HPG)],
            axis=0)
        d_rows = jnp.concatenate(
            [jnp.broadcast_to(e_last_t[g * SSM_HPG + j:g * SSM_HPG + j + 1, :], (SSM_HEAD_DIM, LANES))
             for j in range(SSM_HPG)], axis=0)
        st = _dot((xs_t * w_rows).astype(BF16), bg)
        h_ref[g * gw:(g + 1) * gw, :] = hg * d_rows + st

    y = jnp.concatenate(y_groups, axis=1)
    y = y + xs * dexp_ref[...]
    y = y * _silu(z_ref[...])
    gsz = SSM_INNER // SSM_GROUPS
    outs = []
    for g in range(SSM_GROUPS):
        yg = y[:, g * gsz:(g + 1) * gsz]
        ms = jnp.mean(yg * yg, axis=-1, keepdims=True)
        outs.append(yg * lax.rsqrt(ms + RMS_EPS))
    act_ref[...] = (jnp.concatenate(outs, axis=1) * nw_ref[...]).astype(act_ref.dtype)

    @pl.when(c == n_chunks - 1)
    def _():
        ht_ref[...] = h_ref[...]


def _ssd_core(z, xbc, dt_raw, carry0, h0, conv_w, conv_b, dt_bias, a_log, d_exp, norm_w, ht_buf, *, nseq, rows,
              n_chunks, t_valid, seq0=0):
    kern = functools.partial(_ssd_kernel, rows=rows, n_chunks=n_chunks, t_valid=t_valid)
    rmap = lambda b, c: (b * n_chunks + c, 0)
    const = lambda b, c: (0, 0)
    return pl.pallas_call(
        kern,
        grid=(nseq, n_chunks),
        in_specs=[
            pl.BlockSpec((rows, SSM_INNER), rmap),
            pl.BlockSpec((rows, SSM_CONV_DIM), rmap),
            pl.BlockSpec((rows, LANES), rmap),
            pl.BlockSpec((None, SUBLANES, SSM_CONV_DIM), lambda b, c: (b, 0, 0)),
            pl.BlockSpec((None, SSM_INNER, SSM_STATE), lambda b, c: (seq0 + b, 0, 0)),
            pl.BlockSpec((SSM_CONV, SSM_CONV_DIM), const),
            pl.BlockSpec((1, SSM_CONV_DIM), const),
            pl.BlockSpec((1, LANES), const),
            pl.BlockSpec((1, LANES), const),
            pl.BlockSpec((1, SSM_INNER), const),
            pl.BlockSpec((1, SSM_INNER), const),
            pl.BlockSpec(memory_space=pl.ANY),
        ],
        out_specs=[
            pl.BlockSpec((rows, SSM_INNER), rmap),
            pl.BlockSpec((None, SSM_INNER, SSM_STATE), lambda b, c: (seq0 + b, 0, 0)),
        ],
        out_shape=[
            jax.ShapeDtypeStruct((nseq * n_chunks * rows, SSM_INNER), BF16),
            jax.ShapeDtypeStruct(ht_buf.shape, F32),
        ],
        scratch_shapes=[
            pltpu.VMEM((rows + SUBLANES, SSM_CONV_DIM), F32),
            pltpu.VMEM((SSM_INNER, SSM_STATE), F32),
        ],
        input_output_aliases={11: 1},
        compiler_params=_cparams("arbitrary", "arbitrary"),
        name="ssd_core",
    )(z, xbc, dt_raw, carry0, h0, conv_w, conv_b, dt_bias, a_log, d_exp, norm_w, ht_buf)


def _lru_kernel(gin_ref, xr_ref, carry0_ref, h0_ref, cw_ref, cb_ref, wa_ref, ba_ref, wi_ref, bi_ref, lam_ref,
                act_ref, ht_ref, xp_ref, hc_ref, *, rows, n_chunks, t_valid):
    c = pl.program_id(1)

    @pl.when(c == 0)
    def _():
        hc_ref[...] = h0_ref[...]

    xc = _causal_conv(xp_ref, xr_ref[...], cw_ref, cb_ref[...], carry0_ref, LRU_CONV, rows, c == 0)
    ra = []
    ia = []
    for h in range(LRU_HEADS):
        xh = xc[:, h * LRU_BLOCK:(h + 1) * LRU_BLOCK].astype(BF16)
        ra.append(_dot(xh, wa_ref[h].astype(BF16)))
        ia.append(_dot(xh, wi_ref[h].astype(BF16)))
    r = jax.nn.sigmoid(jnp.concatenate(ra, axis=1) + ba_ref[...])
    ig = jax.nn.sigmoid(jnp.concatenate(ia, axis=1) + bi_ref[...])
    log_a = -LRU_C * r * _softplus(-lam_ref[...])
    a = jnp.exp(log_a)
    th = jnp.tanh(log_a)
    mult = jnp.sqrt(-2.0 * th / (1.0 - th))
    u = mult * (ig * xc)
    if t_valid < rows:
        row = lax.broadcasted_iota(jnp.int32, (rows, LRU_WIDTH), 0)
        a = jnp.where(row < t_valid, a, 1.0)
        u = jnp.where(row < t_valid, u, 0.0)

    sub = lax.broadcasted_iota(jnp.int32, (SUBLANES, LRU_WIDTH), 0)
    h_prev = hc_ref[...]
    h_blocks = []
    for blk in range(rows // SUBLANES):
        ab = a[blk * SUBLANES:(blk + 1) * SUBLANES, :]
        ub = u[blk * SUBLANES:(blk + 1) * SUBLANES, :]
        for d in (1, 2, 4):
            a_sh = jnp.where(sub >= d, pltpu.roll(ab, d, 0), 1.0)
            u_sh = jnp.where(sub >= d, pltpu.roll(ub, d, 0), 0.0)
            ub = ub + ab * u_sh
            ab = ab * a_sh
        hb = ub + ab * h_prev
        h_blocks.append(hb)
        h_prev = hb[SUBLANES - 1:SUBLANES, :]
    hc_ref[...] = h_prev
    hseq = jnp.concatenate(h_blocks, axis=0)
    gx = gin_ref[...]
    gate = 0.5 * gx * (1.0 + jnp.tanh(math.sqrt(2.0 / math.pi) * (gx + 0.044715 * (gx * gx * gx))))
    act_ref[...] = (hseq * gate).astype(act_ref.dtype)

    @pl.when(c == n_chunks - 1)
    def _():
        ht_ref[...] = hc_ref[...]


def _lru_core(proj, carry0, h0, conv_w, conv_b, wa, ba, wi, bi, lam, *, nseq, rows, n_chunks, t_valid):
    kern = functools.partial(_lru_kernel, rows=rows, n_chunks=n_chunks, t_valid=t_valid)
    w = LRU_WIDTH
    const = lambda b, c: (0, 0)
    const3 = lambda b, c: (0, 0, 0)
    return pl.pallas_call(
        kern,
        grid=(nseq, n_chunks),
        in_specs=[
            pl.BlockSpec((rows, w), lambda b, c: (b * n_chunks + c, 0)),
            pl.BlockSpec((rows, w), lambda b, c: (b * n_chunks + c, 1)),
            pl.BlockSpec((None, SUBLANES, w), lambda b, c: (b, 0, 0)),
            pl.BlockSpec((None, 1, w), lambda b, c: (b, 0, 0)),
            pl.BlockSpec((LRU_CONV, w), const),
            pl.BlockSpec((1, w), const),
            pl.BlockSpec((LRU_HEADS, LRU_BLOCK, LRU_BLOCK), const3),
            pl.BlockSpec((1, w), const),
            pl.BlockSpec((LRU_HEADS, LRU_BLOCK, LRU_BLOCK), const3),
            pl.BlockSpec((1, w), const),
            pl.BlockSpec((1, w), const),
        ],
        out_specs=[
            pl.BlockSpec((rows, w), lambda b, c: (b * n_chunks + c, 0)),
            pl.BlockSpec((None, 1, w), lambda b, c: (b, 0, 0)),
        ],
        out_shape=[
            jax.ShapeDtypeStruct((nseq * n_chunks * rows, w), BF16),
            jax.ShapeDtypeStruct((nseq, 1, w), F32),
        ],
        scratch_shapes=[pltpu.VMEM((rows + SUBLANES, w), F32), pltpu.VMEM((1, w), F32)],
        compiler_params=_cparams("parallel", "arbitrary"),
        name="lru_core",
    )(proj, proj, carry0, h0, conv_w, conv_b, wa, ba, wi, bi, lam)


def _sc_kernel(bg_ref, cg_ref, hh_ref, carry0_ref, cw_ref, act_ref, vt_ref, xp_ref, *, rows, n_chunks, t_valid):
    c = pl.program_id(1)
    v = cg_ref[...] * hh_ref[...]
    u = _causal_conv(xp_ref, v, cw_ref, None, carry0_ref, SC_WIDTH, rows, c == 0)
    act_ref[...] = (bg_ref[...] * u).astype(act_ref.dtype)
    lo = max(t_valid - SUBLANES, 0)

    @pl.when(c == n_chunks - 1)
    def _():
        vt_ref[...] = v[lo:lo + SUBLANES, :]


def _sc_core(proj, carry0, conv_w, *, nseq, rows, n_chunks, t_valid):
    kern = functools.partial(_sc_kernel, rows=rows, n_chunks=n_chunks, t_valid=t_valid)
    d = D_MODEL
    return pl.pallas_call(
        kern,
        grid=(nseq, n_chunks),
        in_specs=[
            pl.BlockSpec((rows, d), lambda b, c: (b * n_chunks + c, 0)),
            pl.BlockSpec((rows, d), lambda b, c: (b * n_chunks + c, 1)),
            pl.BlockSpec((rows, d), lambda b, c: (b * n_chunks + c, 2)),
            pl.BlockSpec((None, SUBLANES, d), lambda b, c: (b, 0, 0)),
            pl.BlockSpec((SC_WIDTH, d), lambda b, c: (0, 0)),
        ],
        out_specs=[
            pl.BlockSpec((rows, d), lambda b, c: (b * n_chunks + c, 0)),
            pl.BlockSpec((None, SUBLANES, d), lambda b, c: (b, 0, 0)),
        ],
        out_shape=[
            jax.ShapeDtypeStruct((nseq * n_chunks * rows, d), BF16),
            jax.ShapeDtypeStruct((nseq, SUBLANES, d), F32),
        ],
        scratch_shapes=[pltpu.VMEM((rows + SUBLANES, d), F32)],
        compiler_params=_cparams("parallel", "arbitrary"),
        name="sconv_core",
    )(proj, proj, proj, carry0, conv_w)


def _route_kernel(lg_ref, info_ref, w12_ref, cnt_ref, carry_ref, *, tm, n_tiles):
    i = pl.program_id(0)

    @pl.when(i == 0)
    def _():
        carry_ref[...] = jnp.zeros_like(carry_ref)

    lg = lg_ref[...]
    lane = lax.broadcasted_iota(jnp.int32, (tm, LANES), 1).astype(F32)
    neg = -jnp.inf
    big = float(LANES)
    is_g = lane < MOE_GROUPS
    gl = jnp.where(is_g, lg, neg)
    gmax = jnp.max(gl, axis=1, keepdims=True)
    gidx = jnp.min(jnp.where(gl == gmax, lane, big), axis=1, keepdims=True)
    gsum = jnp.sum(jnp.where(is_g, jnp.exp(lg - gmax), 0.0), axis=1, keepdims=True)
    gw = 1.0 / gsum
    rel = lane - (MOE_GROUPS + MOE_PER_GROUP * gidx)
    in_grp = (rel >= 0.0) & (rel < MOE_PER_GROUP)
    el = jnp.where(in_grp, lg, neg)
    v1 = jnp.max(el, axis=1, keepdims=True)
    i1 = jnp.min(jnp.where(in_grp & (el == v1), lane, big), axis=1, keepdims=True)
    in_grp2 = in_grp & (lane != i1)
    el2 = jnp.where(in_grp2, lg, neg)
    v2 = jnp.max(el2, axis=1, keepdims=True)
    i2 = jnp.min(jnp.where(in_grp2 & (el2 == v2), lane, big), axis=1, keepdims=True)
    t = jnp.exp(v2 - v1)
    den = 1.0 + t
    w1 = (1.0 / den) * gw
    w2 = (t / den) * gw
    e1 = i1 - MOE_GROUPS
    e2 = i2 - MOE_GROUPS
    onehot = jnp.where(lane == e1, 1.0, jnp.where(lane == e2, 1.0, 0.0))
    r_i = lax.broadcasted_iota(jnp.int32, (tm, tm), 0)
    c_i = lax.broadcasted_iota(jnp.int32, (tm, tm), 1)
    tri = jnp.where(c_i < r_i, 1.0, 0.0).astype(BF16)
    before = _dot(tri, onehot.astype(BF16)) + carry_ref[0:1, :]
    r1 = jnp.sum(jnp.where(lane == e1, before, 0.0), axis=1, keepdims=True)
    r2 = jnp.sum(jnp.where(lane == e2, before, 0.0), axis=1, keepdims=True)
    carry_ref[0:1, :] = carry_ref[0:1, :] + jnp.sum(onehot, axis=0, keepdims=True)
    info_ref[...] = jnp.where(lane == 0, e1, jnp.where(lane == 1, e2, jnp.where(lane == 2, r1, jnp.where(
        lane == 3, r2, 0.0))))
    w12_ref[...] = jnp.concatenate(
        [jnp.broadcast_to(w1, (tm, LANES)), jnp.broadcast_to(w2, (tm, LANES))], axis=1)

    @pl.when(i == n_tiles - 1)
    def _():
        cnt_ref[...] = carry_ref[...]


def _route(logits, tm):
    n = logits.shape[0]
    n_tiles = n // tm
    kern = functools.partial(_route_kernel, tm=tm, n_tiles=n_tiles)
    return pl.pallas_call(
        kern,
        grid=(n_tiles,),
        in_specs=[pl.BlockSpec((tm, LANES), lambda i: (i, 0))],
        out_specs=[
            pl.BlockSpec((tm, LANES), lambda i: (i, 0)),
            pl.BlockSpec((tm, 2 * LANES), lambda i: (i, 0)),
            pl.BlockSpec((SUBLANES, LANES), lambda i: (0, 0)),
        ],
        out_shape=[
            jax.ShapeDtypeStruct((n, LANES), F32),
            jax.ShapeDtypeStruct((n, 2 * LANES), F32),
            jax.ShapeDtypeStruct((SUBLANES, LANES), F32),
        ],
        scratch_shapes=[pltpu.VMEM((SUBLANES, LANES), F32)],
        compiler_params=_cparams("arbitrary"),
        name="route",
    )(logits)


def _dispatch_kernel(dest_ref, x_ref, nw_ref, xg_ref, xn_ref, sem, *, tm):
    base = pl.program_id(0) * tm
    x = x_ref[...]
    ms = jnp.mean(x * x, axis=-1, keepdims=True)
    xn_ref[...] = x * lax.rsqrt(ms + RMS_EPS) * nw_ref[...]

    def row_copy(r, d):
        return pltpu.make_async_copy(xn_ref.at[pl.ds(r, 1), :], xg_ref.at[pl.ds(d, 1), :], sem)

    def issue(r, carry):
        row_copy(r, dest_ref[2 * (base + r)]).start(priority=0)
        row_copy(r, dest_ref[2 * (base + r) + 1]).start(priority=1)
        return carry

    lax.fori_loop(0, tm, issue, 0, unroll=8)

    def drain(r, carry):
        row_copy(0, 0).wait()
        row_copy(0, 0).wait()
        return carry

    lax.fori_loop(0, tm, drain, 0, unroll=8)


def _dispatch(x1, norm_w, dest_flat, tm):
    n, d = x1.shape
    grid_spec = pltpu.PrefetchScalarGridSpec(
        num_scalar_prefetch=1,
        grid=(n // tm,),
        in_specs=[pl.BlockSpec((tm, d), lambda i, dest: (i, 0)), pl.BlockSpec((1, d), lambda i, dest: (0, 0))],
        out_specs=pl.BlockSpec(memory_space=pl.ANY),
        scratch_shapes=[pltpu.VMEM((tm, d), F32), pltpu.SemaphoreType.DMA(())],
    )
    return pl.pallas_call(
        functools.partial(_dispatch_kernel, tm=tm),
        grid_spec=grid_spec,
        out_shape=jax.ShapeDtypeStruct((2 * n, d), F32),
        compiler_params=_cparams("arbitrary"),
        name="moe_dispatch",
    )(dest_flat, x1, norm_w.reshape(1, d))


VISIT_VALID = 1
VISIT_FIRST_OF_EXPERT = 2
VISIT_SLOT = 4


def _ffn_kernel(vt_ref, ve_ref, vf_ref, vn_ref, off_ref, x_ref, wg_hbm, wu_hbm, wd_hbm, o_ref,
                sg_ref, su_ref, sd_ref, cg_ref, cu_ref, cd_ref, sem, *, tm, layer):
    v = pl.program_id(0)
    t = vt_ref[v]
    e = ve_ref[v]
    flags = vf_ref[v]
    first_of_tile = jnp.logical_or(v == 0, vt_ref[jnp.maximum(v - 1, 0)] != t)

    @pl.when(first_of_tile)
    def _():
        o_ref[...] = jnp.zeros_like(o_ref)

    def copies(expert, slot):
        return (pltpu.make_async_copy(wg_hbm.at[layer, expert], sg_ref.at[slot], sem.at[slot, 0]),
                pltpu.make_async_copy(wu_hbm.at[layer, expert], su_ref.at[slot], sem.at[slot, 1]),
                pltpu.make_async_copy(wd_hbm.at[layer, expert], sd_ref.at[slot], sem.at[slot, 2]))

    @pl.when((flags & VISIT_FIRST_OF_EXPERT) != 0)
    def _():
        slot = (flags & VISIT_SLOT) // VISIT_SLOT
        nxt = vn_ref[v]

        @pl.when(v == 0)
        def _():
            for c in copies(e, slot):
                c.start()

        @pl.when(nxt >= 0)
        def _():
            for c in copies(nxt, 1 - slot):
                c.start()

        for c in copies(e, slot):
            c.wait()
        cg_ref[...] = sg_ref[slot].astype(BF16)
        cu_ref[...] = su_ref[slot].astype(BF16)
        cd_ref[...] = sd_ref[slot].astype(BF16)

    @pl.when((flags & VISIT_VALID) != 0)
    def _():
        row = t * tm + lax.broadcasted_iota(jnp.int32, (tm, MOE_FF), 0)
        mine = (row >= off_ref[e]) & (row < off_ref[e + 1])
        x = x_ref[...].astype(BF16)
        hg = _dot(x, cg_ref[...])
        hu = _dot(x, cu_ref[...])
        hid = jnp.where(mine, _silu(hg) * hu, 0.0)
        o_ref[...] += _dot(hid.astype(BF16), cd_ref[...])


def _ffn(xg, w_gate, w_up, w_down, layer, plan, tm):
    rows, d = xg.shape
    n_visits = plan[0].shape[0]
    kern = functools.partial(_ffn_kernel, tm=tm, layer=layer)
    grid_spec = pltpu.PrefetchScalarGridSpec(
        num_scalar_prefetch=5,
        grid=(n_visits,),
        in_specs=[
            pl.BlockSpec((tm, d), lambda v, vt, *_: (vt[v], 0)),
            pl.BlockSpec(memory_space=pl.ANY),
            pl.BlockSpec(memory_space=pl.ANY),
            pl.BlockSpec(memory_space=pl.ANY),
        ],
        out_specs=pl.BlockSpec((tm, d), lambda v, vt, *_: (vt[v], 0)),
        scratch_shapes=[
            pltpu.VMEM((2, d, MOE_FF), F32), pltpu.VMEM((2, d, MOE_FF), F32), pltpu.VMEM((2, MOE_FF, d), F32),
            pltpu.VMEM((d, MOE_FF), BF16), pltpu.VMEM((d, MOE_FF), BF16), pltpu.VMEM((MOE_FF, d), BF16),
            pltpu.SemaphoreType.DMA((2, 3)),
        ],
    )
    return pl.pallas_call(
        kern,
        grid_spec=grid_spec,
        out_shape=jax.ShapeDtypeStruct((rows, d), F32),
        compiler_params=_cparams("arbitrary"),
        name="moe_ffn",
    )(*plan, xg, w_gate, w_up, w_down)


def _visit_plan(counts, n_tiles, tm):
    off = jnp.concatenate([jnp.zeros((1,), jnp.int32), jnp.cumsum(counts)])
    n_visits = n_tiles + N_EXPERTS - 1
    first_tile = off[:-1] // tm
    last_tile = jnp.where(counts > 0, (off[1:] - 1) // tm, first_tile)
    n_e = jnp.where(counts > 0, last_tile - first_tile + 1, 0)
    v_off = jnp.concatenate([jnp.zeros((1,), jnp.int32), jnp.cumsum(n_e)])
    total = v_off[-1]
    vid = jnp.arange(n_visits, dtype=jnp.int32)
    exp_of = jnp.sum((vid[:, None] >= v_off[None, 1:]).astype(jnp.int32), axis=1)
    exp_of = jnp.minimum(exp_of, N_EXPERTS - 1)
    tile_of = first_tile[exp_of] + (vid - v_off[exp_of])
    valid = vid < total
    first_of_expert = valid & (vid == v_off[exp_of])
    used = counts > 0
    order = jnp.cumsum(used.astype(jnp.int32)) - 1
    ids = jnp.arange(N_EXPERTS, dtype=jnp.int32)
    later = jnp.where(used[None, :] & (ids[None, :] > ids[:, None]), ids[None, :], N_EXPERTS)
    next_used = jnp.min(later, axis=1)
    next_used = jnp.where(next_used < N_EXPERTS, next_used, -1)
    last_e = jnp.minimum(jnp.sum((total - 1 >= v_off[1:]).astype(jnp.int32)), N_EXPERTS - 1)
    exp_of = jnp.where(valid, exp_of, last_e)
    tile_of = jnp.where(valid, tile_of, n_tiles - 1)
    flags = valid * VISIT_VALID + first_of_expert * VISIT_FIRST_OF_EXPERT + (order[exp_of] % 2) * VISIT_SLOT
    i32 = lambda a: a.astype(jnp.int32)
    return i32(tile_of), i32(exp_of), i32(flags), i32(next_used[exp_of]), i32(off)


def _combine_kernel(dest_ref, x1_ref, w12_ref, nw_ref, yg_ref, *refs, tm, n_tiles, n_prompt_tiles, final):
    outs, (buf, sem) = refs[:-2], refs[-2:]
    i = pl.program_id(0)

    def row_copy(slot, k, r, d):
        return pltpu.make_async_copy(yg_ref.at[pl.ds(d, 1), :], buf.at[slot, k, pl.ds(r, 1), :], sem.at[slot])

    def issue(tile, slot):
        base = tile * tm

        def body(r, carry):
            row_copy(slot, 0, r, dest_ref[2 * (base + r)]).start(priority=0)
            row_copy(slot, 1, r, dest_ref[2 * (base + r) + 1]).start(priority=1)
            return carry

        lax.fori_loop(0, tm, body, 0, unroll=8)

    @pl.when(i == 0)
    def _():
        issue(0, 0)

    @pl.when(i + 1 < n_tiles)
    def _():
        issue(i + 1, lax.rem(i + 1, 2))

    slot = lax.rem(i, 2)

    def drain(r, carry):
        row_copy(slot, 0, 0, 0).wait()
        row_copy(slot, 1, 0, 0).wait()
        return carry

    lax.fori_loop(0, tm, drain, 0, unroll=8)

    w12 = w12_ref[...]
    reps = x1_ref.shape[1] // LANES
    w1 = jnp.concatenate([w12[:, :LANES]] * reps, axis=1)
    w2 = jnp.concatenate([w12[:, LANES:]] * reps, axis=1)
    x2 = x1_ref[...] + (w1 * buf[slot, 0] + w2 * buf[slot, 1])
    ms = jnp.mean(x2 * x2, axis=-1, keepdims=True)
    xn = x2 * lax.rsqrt(ms + RMS_EPS) * nw_ref[...]
    if final:
        split, split_val = outs, xn
    else:
        split, split_val = outs[:2], x2
        outs[2][...] = xn.astype(outs[2].dtype)

    @pl.when(i < n_prompt_tiles)
    def _():
        split[0][...] = split_val

    @pl.when(i >= n_prompt_tiles)
    def _():
        split[1][...] = split_val


def _combine(x1, w12, yg, dest_flat, norm_w, n_p, tm, final):
    n, d = x1.shape
    n_tiles = n // tm
    npt = n_p // tm
    spec_p = pl.BlockSpec((tm, d), lambda i, dest: (jnp.minimum(i, npt - 1), 0))
    spec_s = pl.BlockSpec((tm, d), lambda i, dest: (jnp.maximum(i - npt, 0), 0))
    out_specs = [spec_p, spec_s]
    out_shape = [jax.ShapeDtypeStruct((n_p, d), F32), jax.ShapeDtypeStruct((n - n_p, d), F32)]
    if not final:
        out_specs.append(pl.BlockSpec((tm, d), lambda i, dest: (i, 0)))
        out_shape.append(jax.ShapeDtypeStruct((n, d), BF16))
    grid_spec = pltpu.PrefetchScalarGridSpec(
        num_scalar_prefetch=1,
        grid=(n_tiles,),
        in_specs=[
            pl.BlockSpec((tm, d), lambda i, dest: (i, 0)),
            pl.BlockSpec((tm, 2 * LANES), lambda i, dest: (i, 0)),
            pl.BlockSpec((1, d), lambda i, dest: (0, 0)),
            pl.BlockSpec(memory_space=pl.ANY),
        ],
        out_specs=out_specs,
        scratch_shapes=[pltpu.VMEM((2, 2, tm, d), F32), pltpu.SemaphoreType.DMA((2,))],
    )
    return pl.pallas_call(
        functools.partial(_combine_kernel, tm=tm, n_tiles=n_tiles, n_prompt_tiles=npt, final=final),
        grid_spec=grid_spec,
        out_shape=out_shape,
        compiler_params=_cparams("arbitrary"),
        name="moe_combine",
    )(dest_flat, x1, w12, norm_w.reshape(1, d), yg)


def _pad_lanes(v, width=LANES):
    v = v.reshape(1, -1)
    return jnp.pad(v, ((0, 0), (0, width - v.shape[1])))


def _sample_rows(m, n_p, b_s, t_s):
    c = m.shape[1]
    s = m[n_p:].reshape(b_s, t_s, c)
    s = jnp.pad(s, ((0, 0), (0, SAMPLE_PAD_T - t_s), (0, 0)))
    return s.reshape(b_s * SAMPLE_PAD_T, c)


def _sample_act(act, b_s, t_s):
    c = act.shape[1]
    return act.reshape(b_s, SAMPLE_PAD_T, c)[:, :t_s].reshape(b_s * t_s, c)


def _carry_rows(state, width):
    return jnp.pad(state, ((0, 0), (SUBLANES - (width - 1), 0), (0, 0)))


def _moe(x1, ffn_norm_w, logits, layer, w_gate, w_up, w_down, next_norm_w, n_p, final):
    n = x1.shape[0]
    info, w12, cnt = _route(logits, ROUTE_TM)
    counts = cnt[0, :N_EXPERTS].astype(jnp.int32)
    e12 = info[:, 0:2].astype(jnp.int32)
    r12 = info[:, 2:4].astype(jnp.int32)
    plan = _visit_plan(counts, (2 * n) // FFN_TM, FFN_TM)
    off = plan[-1]
    dest = (off[e12] + r12).reshape(-1)
    xg = _dispatch(x1, ffn_norm_w, dest, DISPATCH_TM)
    yg = _ffn(xg, w_gate, w_up, w_down, layer, plan, FFN_TM)
    return _combine(x1, w12, yg, dest, next_norm_w, n_p, COMBINE_TM, final)


def kernel(x_prompt, x_sample, state_ssm_conv, state_ssm, state_lru_conv, state_lru, state_sconv, norm_mix, norm_ffn, norm_final, w_ssm_in, ssm_conv_w, ssm_conv_b, ssm_dt_bias, ssm_A_log, ssm_D, ssm_norm_w, w_ssm_out, w_lru_in, lru_conv_w, lru_conv_b, lru_wa, lru_ba, lru_wi, lru_bi, lru_lambda, w_lru_out, w_sc_in, sc_conv_w, w_sc_out, w_route_group, w_route_expert, w_exp_gate, w_exp_up, w_exp_down):
    b_p, t_p, d = x_prompt.shape
    b_s, t_s, _ = x_sample.shape
    n_p = b_p * t_p
    n_s = b_s * t_s
    n = n_p + n_s
    assert n_s == ROW_TM and n_p % ROW_TM == 0 and t_p % SSM_CHUNK == 0 and t_s <= SAMPLE_PAD_T
    mm_tm = n // 4
    n_chunks_p = t_p // SSM_CHUNK
    n_a = state_ssm.shape[0]

    x_p = x_prompt.reshape(n_p, d)
    x_s = x_sample.reshape(n_s, d)
    xn = _rmsnorm(x_p, x_s, norm_mix[0], ROW_TM, BF16)

    def last_rows_p(m, col0, ncol, k):
        return jnp.stack([m[(b + 1) * t_p - k:(b + 1) * t_p, col0:col0 + ncol] for b in range(b_p)])

    def last_rows_s(state, m, col0, ncol):
        new = m[n_p:, col0:col0 + ncol].reshape(b_s, t_s, ncol)
        return jnp.concatenate([state, new], axis=1)[:, t_s:]

    ssd_h_p = jnp.zeros((n_a * b_p, SSM_INNER, SSM_STATE), F32)
    ssd_h_s = jnp.zeros((n_a * b_s, SSM_INNER, SSM_STATE), F32)
    ssd_h0_p = jnp.zeros((n_a * b_p, SSM_INNER, SSM_STATE), F32)
    ssd_h0_s = state_ssm.reshape(n_a * b_s, SSM_INNER, SSM_STATE)
    w_ssm_in_t = jnp.swapaxes(w_ssm_in, 1, 2)

    p_ssm_conv, p_lru_conv, p_lru, p_sconv = [], [], [], []
    s_ssm_conv, s_lru_conv, s_lru, s_sconv = [], [], [], []

    for i in range(DEPTH):
        kind = i % N_MIXERS
        s = i // N_MIXERS
        w_r = jnp.pad(jnp.concatenate([w_route_group[i], w_route_expert[i]], axis=1),
                      ((0, 0), (0, LANES - MOE_GROUPS - N_EXPERTS)))
        r_hi = w_r.astype(BF16)
        r_lo = (w_r - r_hi.astype(F32)).astype(BF16)
        if kind == 0:
            z = _matmul(xn, w_ssm_in_t, s, 0, SSM_INNER, PROJ_TN, mm_tm, transposed=True)
            xbc = _matmul(xn, w_ssm_in_t, s, SSM_INNER // PROJ_TN, SSM_CONV_DIM, PROJ_TN, mm_tm, transposed=True)
            w_dt = jnp.pad(w_ssm_in_t[s, SSM_INNER + SSM_CONV_DIM:, :], ((0, LANES - SSM_HEADS), (0, 0)))[None]
            dt_raw = _matmul(xn, w_dt, 0, 0, LANES, LANES, mm_tm, transposed=True)
            params = (ssm_conv_w[s], ssm_conv_b[s].reshape(1, -1), _pad_lanes(ssm_dt_bias[s]),
                      _pad_lanes(ssm_A_log[s]), jnp.repeat(ssm_D[s], SSM_HEAD_DIM).reshape(1, -1),
                      ssm_norm_w[s].reshape(1, -1))
            act_p, ssd_h_p = _ssd_core(
                z, xbc, dt_raw, jnp.zeros((b_p, SUBLANES, SSM_CONV_DIM), F32), ssd_h0_p, *params, ssd_h_p,
                nseq=b_p, rows=SSM_CHUNK, n_chunks=n_chunks_p, t_valid=SSM_CHUNK, seq0=s * b_p)
            act_s, ssd_h_s = _ssd_core(
                _sample_rows(z, n_p, b_s, t_s), _sample_rows(xbc, n_p, b_s, t_s), _sample_rows(dt_raw, n_p, b_s, t_s),
                _carry_rows(state_ssm_conv[s], SSM_CONV), ssd_h0_s, *params, ssd_h_s,
                nseq=b_s, rows=SAMPLE_PAD_T, n_chunks=1, t_valid=t_s, seq0=s * b_s)
            act_s = _sample_act(act_s, b_s, t_s)
            p_ssm_conv.append(last_rows_p(xbc, 0, SSM_CONV_DIM, SSM_CONV - 1))
            s_ssm_conv.append(last_rows_s(state_ssm_conv[s], xbc, 0, SSM_CONV_DIM))
            w_out = w_ssm_out
        elif kind == 1:
            proj = _matmul(xn, w_lru_in, s, 0, 2 * LRU_WIDTH, PROJ_TN, mm_tm)
            params = (lru_conv_w[s], lru_conv_b[s].reshape(1, -1), lru_wa[s], lru_ba[s].reshape(1, -1), lru_wi[s],
                      lru_bi[s].reshape(1, -1), lru_lambda[s].reshape(1, -1))
            act_p, h_p = _lru_core(
                proj, jnp.zeros((b_p, SUBLANES, LRU_WIDTH), F32), jnp.zeros((b_p, 1, LRU_WIDTH), F32), *params,
                nseq=b_p, rows=SSM_CHUNK, n_chunks=n_chunks_p, t_valid=SSM_CHUNK)
            act_s, h_s = _lru_core(
                _sample_rows(proj, n_p, b_s, t_s), _carry_rows(state_lru_conv[s], LRU_CONV),
                state_lru[s].reshape(b_s, 1, LRU_WIDTH), *params,
                nseq=b_s, rows=SAMPLE_PAD_T, n_chunks=1, t_valid=t_s)
            act_s = _sample_act(act_s, b_s, t_s)
            p_lru_conv.append(last_rows_p(proj, LRU_WIDTH, LRU_WIDTH, LRU_CONV - 1))
            s_lru_conv.append(last_rows_s(state_lru_conv[s], proj, LRU_WIDTH, LRU_WIDTH))
            p_lru.append(h_p.reshape(b_p, LRU_WIDTH))
            s_lru.append(h_s.reshape(b_s, LRU_WIDTH))
            w_out = w_lru_out
        else:
            proj = _matmul(xn, w_sc_in, s, 0, 3 * d, PROJ_TN, mm_tm)
            act_p, v_p = _sc_core(proj, jnp.zeros((b_p, SUBLANES, d), F32), sc_conv_w[s],
                                  nseq=b_p, rows=SSM_CHUNK, n_chunks=n_chunks_p, t_valid=SSM_CHUNK)
            act_s, v_s = _sc_core(_sample_rows(proj, n_p, b_s, t_s), _carry_rows(state_sconv[s], SC_WIDTH),
                                  sc_conv_w[s], nseq=b_s, rows=SAMPLE_PAD_T, n_chunks=1, t_valid=t_s)
            act_s = _sample_act(act_s, b_s, t_s)
            p_sconv.append(v_p[:, SUBLANES - (SC_WIDTH - 1):])
            s_sconv.append(jnp.concatenate([state_sconv[s], v_s[:, :t_s]], axis=1)[:, t_s:])
            w_out = w_sc_out

        x1, logits = _outproj(act_p, act_s, w_out[s].astype(BF16), x_p, x_s, norm_ffn[i], r_hi, r_lo, PROJ_TK)
        moe_w = (w_exp_gate, w_exp_up, w_exp_down)
        if i < DEPTH - 1:
            x_p, x_s, xn = _moe(x1, norm_ffn[i], logits, i, *moe_w, norm_mix[i + 1], n_p, False)
        else:
            y_p, y_s = _moe(x1, norm_ffn[i], logits, i, *moe_w, norm_final, n_p, True)

    state_shape = (SSM_HEADS, SSM_HEAD_DIM, SSM_STATE)
    return (y_p.reshape(b_p, t_p, d), y_s.reshape(b_s, t_s, d),
            jnp.stack(p_ssm_conv), ssd_h_p.reshape((n_a, b_p) + state_shape), jnp.stack(p_lru_conv), jnp.stack(p_lru),
            jnp.stack(p_sconv),
            jnp.stack(s_ssm_conv), ssd_h_s.reshape((n_a, b_s) + state_shape), jnp.stack(s_lru_conv), jnp.stack(s_lru),
            jnp.stack(s_sconv))
```

```python
import functools
import math

import jax
import jax.numpy as jnp
from jax import lax
from jax.experimental import pallas as pl
from jax.experimental.pallas import tpu as pltpu

F32 = jnp.float32
BF16 = jnp.bfloat16

D_MODEL = 2048
DEPTH = 4
N_MIXERS = 3
RMS_EPS = 1e-6
LOG2_E = 1.4426950408889634
SSM_INNER = 4096
SSM_HEAD_DIM = 64
SSM_HEADS = 64
SSM_GROUPS = 8
SSM_HPG = 8
SSM_STATE = 128
SSM_CONV = 4
SSM_BC = SSM_GROUPS * SSM_STATE
SSM_CONV_DIM = SSM_INNER + 2 * SSM_BC
SSM_CHUNK = 128
LRU_WIDTH = 2048
LRU_HEADS = 16
LRU_BLOCK = 128
LRU_CONV = 4
LRU_C = 8.0
SC_WIDTH = 3
MOE_GROUPS = 8
MOE_PER_GROUP = 4
N_EXPERTS = 32
MOE_FF = 512

LANES = 128
SUBLANES = 8
SAMPLE_PAD_T = 16
VMEM_LIMIT = 56 * 1024 * 1024

ROW_TM = 512
PROJ_TN = 512
PROJ_TK = 1024
ROUTE_TM = 512
DISPATCH_TM = 512
FFN_TM = 256
COMBINE_TM = 256


def _cparams(*sem):
    return pltpu.CompilerParams(dimension_semantics=sem, vmem_limit_bytes=VMEM_LIMIT)


def _silu(x):
    return x * jax.nn.sigmoid(x)


def _softplus(x):
    return jnp.maximum(x, 0.0) + jnp.log1p(jnp.exp(-jnp.abs(x)))


def _dot(a, b):
    return jnp.dot(a, b, preferred_element_type=F32)


def _dot_nt(a, b):
    return lax.dot_general(a, b, (((1,), (1,)), ((), ())), preferred_element_type=F32)


def _split3(x):
    hi = x.astype(BF16)
    r = x - hi.astype(F32)
    mid = r.astype(BF16)
    lo = (r - mid.astype(F32)).astype(BF16)
    return hi, mid, lo


def _norm_kernel(xp_ref, xs_ref, w_ref, o_ref, *, n_prompt_tiles):
    def emit(x):
        ms = jnp.mean(x * x, axis=-1, keepdims=True)
        o_ref[...] = (x * lax.rsqrt(ms + RMS_EPS) * w_ref[...]).astype(o_ref.dtype)

    @pl.when(pl.program_id(0) < n_prompt_tiles)
    def _():
        emit(xp_ref[...])

    @pl.when(pl.program_id(0) >= n_prompt_tiles)
    def _():
        emit(xs_ref[...])


def _rmsnorm(x_p, x_s, w, tm, out_dtype):
    n_p, d = x_p.shape
    n = n_p + x_s.shape[0]
    npt = n_p // tm
    return pl.pallas_call(
        functools.partial(_norm_kernel, n_prompt_tiles=npt),
        grid=(n // tm,),
        in_specs=[
            pl.BlockSpec((tm, d), lambda i: (jnp.minimum(i, npt - 1), 0)),
            pl.BlockSpec((tm, d), lambda i: (jnp.maximum(i - npt, 0), 0)),
            pl.BlockSpec((1, d), lambda i: (0, 0)),
        ],
        out_specs=pl.BlockSpec((tm, d), lambda i: (i, 0)),
        out_shape=jax.ShapeDtypeStruct((n, d), out_dtype),
        compiler_params=_cparams("arbitrary"),
        name="rmsnorm",
    )(x_p, x_s, w.reshape(1, d))


def _mm_kernel(a_ref, w_ref, o_ref, *, transposed):
    w = w_ref[...].astype(BF16)
    o_ref[...] = _dot_nt(a_ref[...], w) if transposed else _dot(a_ref[...], w)


def _matmul(a, w_stack, layer, col_blk0, ncols, tn, tm, transposed=False):
    n, k = a.shape
    if transposed:
        w_spec = pl.BlockSpec((None, tn, k), lambda i, j: (layer, col_blk0 + j, 0))
    else:
        w_spec = pl.BlockSpec((None, k, tn), lambda i, j: (layer, 0, col_blk0 + j))
    return pl.pallas_call(
        functools.partial(_mm_kernel, transposed=transposed),
        grid=(n // tm, ncols // tn),
        in_specs=[pl.BlockSpec((tm, k), lambda i, j: (i, 0)), w_spec],
        out_specs=pl.BlockSpec((tm, tn), lambda i, j: (i, j)),
        out_shape=jax.ShapeDtypeStruct((n, ncols), F32),
        compiler_params=_cparams("parallel", "arbitrary"),
        name="in_proj",
    )(a, w_stack)


def _outproj_kernel(ap_ref, as_ref, w_ref, xp_ref, xs_ref, nw_ref, rh_ref, rl_ref, x1_ref, lg_ref, acc_ref,
                    *, n_prompt_tiles, nk, tk):
    i = pl.program_id(0)
    k = pl.program_id(1)
    w = w_ref[pl.ds(pl.multiple_of(k * tk, tk), tk), :]

    @pl.when(i < n_prompt_tiles)
    def _():
        @pl.when(k == 0)
        def _():
            acc_ref[...] = xp_ref[...]

        acc_ref[...] += _dot(ap_ref[...], w)

    @pl.when(i >= n_prompt_tiles)
    def _():
        @pl.when(k == 0)
        def _():
            acc_ref[...] = xs_ref[...]

        acc_ref[...] += _dot(as_ref[...], w)

    @pl.when(k == nk - 1)
    def _():
        x1 = acc_ref[...]
        x1_ref[...] = x1
        ms = jnp.mean(x1 * x1, axis=-1, keepdims=True)
        xn = x1 * lax.rsqrt(ms + RMS_EPS) * nw_ref[...]
        hi = xn.astype(BF16)
        lo = (xn - hi.astype(F32)).astype(BF16)
        rh = rh_ref[...]
        lg_ref[...] = _dot(hi, rh) + (_dot(hi, rl_ref[...]) + _dot(lo, rh))


def _outproj(act_p, act_s, w_bf16, x_p, x_s, norm_w, r_hi, r_lo, tk):
    n_p, kdim = act_p.shape
    d = x_p.shape[1]
    tm = act_s.shape[0]
    n = n_p + tm
    npt = n_p // tm
    nk = kdim // tk
    kern = functools.partial(_outproj_kernel, n_prompt_tiles=npt, nk=nk, tk=tk)
    return pl.pallas_call(
        kern,
        grid=(n // tm, nk),
        in_specs=[
            pl.BlockSpec((tm, tk), lambda i, k: (jnp.minimum(i, npt - 1), jnp.where(i < npt, k, nk - 1))),
            pl.BlockSpec((tm, tk), lambda i, k: (0, jnp.where(i >= npt, k, 0))),
            pl.BlockSpec((kdim, d), lambda i, k: (0, 0), pipeline_mode=pl.Buffered(1)),
            pl.BlockSpec((tm, d), lambda i, k: (jnp.minimum(i, npt - 1), 0)),
            pl.BlockSpec((tm, d), lambda i, k: (0, 0), pipeline_mode=pl.Buffered(1)),
            pl.BlockSpec((1, d), lambda i, k: (0, 0)),
            pl.BlockSpec((d, LANES), lambda i, k: (0, 0)),
            pl.BlockSpec((d, LANES), lambda i, k: (0, 0)),
        ],
        out_specs=[
            pl.BlockSpec((tm, d), lambda i, k: (i, 0)),
            pl.BlockSpec((tm, LANES), lambda i, k: (i, 0)),
        ],
        out_shape=[
            jax.ShapeDtypeStruct((n, d), F32),
            jax.ShapeDtypeStruct((n, LANES), F32),
        ],
        scratch_shapes=[pltpu.VMEM((tm, d), F32)],
        compiler_params=_cparams("arbitrary", "arbitrary"),
        name="out_proj",
    )(act_p, act_s, w_bf16, x_p, x_s, norm_w.reshape(1, d), r_hi, r_lo)


def _causal_conv(carry_ref, x, w_ref, bias, carry0_ref, width, rows, first):
    @pl.when(first)
    def _():
        carry_ref[...] = carry0_ref[...]

    n_tiles = rows // SUBLANES
    tiles = [carry_ref[...]] + [x[i * SUBLANES:(i + 1) * SUBLANES, :] for i in range(n_tiles)]
    sub = lax.broadcasted_iota(jnp.int32, tiles[0].shape, 0)
    y = None
    for k in range(width):
        back = width - 1 - k
        if back == 0:
            tap = x
        else:
            rolled = [pltpu.roll(t, back, 0) for t in tiles]
            tap = jnp.concatenate(
                [jnp.where(sub < back, rolled[i], rolled[i + 1]) for i in range(n_tiles)], axis=0)
        term = tap * w_ref[k:k + 1, :]
        y = term if y is None else y + term
    if bias is not None:
        y = y + bias
    carry_ref[...] = tiles[-1]
    return y


def _pad_rows(a, rows):
    if a.shape[0] == rows:
        return a
    return jnp.concatenate([a, jnp.zeros((rows - a.shape[0], a.shape[1]), a.dtype)], axis=0)


def _ssd_kernel(*refs, nseq, n_fill, aliased, **static):
    if aliased:
        refs = refs[:11] + refs[12:]
    if n_fill == 0:
        _ssd_body(*refs, **static)
        return
    ht_ref = refs[12]

    @pl.when(pl.program_id(0) < nseq)
    def _():
        _ssd_body(*refs, **static)

    @pl.when(pl.program_id(0) >= nseq)
    def _():
        ht_ref[...] = jnp.zeros_like(ht_ref)


def _ssd_body(z_ref, xbc_ref, dt_ref, carry0_ref, h0_ref, cw_ref, cb_ref, dtb_ref, alog_ref, dexp_ref, nw_ref,
              act_ref, ht_ref, xp_ref, h_ref, *, rows, n_chunks, t_valid):
    c = pl.program_id(1)
    lp = SSM_CHUNK

    @pl.when(c == 0)
    def _():
        h_ref[...] = h0_ref[...]

    conv = _causal_conv(xp_ref, xbc_ref[...], cw_ref, cb_ref[...], carry0_ref, SSM_CONV, rows, c == 0)
    xbc = _silu(conv)
    xs = xbc[:, :SSM_INNER]
    bm = xbc[:, SSM_INNER:SSM_INNER + SSM_BC]
    cm = xbc[:, SSM_INNER + SSM_BC:]

    row_q = lax.broadcasted_iota(jnp.int32, (rows, LANES), 0)
    dt = _softplus(dt_ref[...] + dtb_ref[...])
    if t_valid < rows:
        dt = jnp.where(row_q < t_valid, dt, 0.0)
    a_neg = -jnp.exp(alog_ref[...])
    dt_p = _pad_rows(dt, lp)
    ad = dt_p * a_neg
    r_i = lax.broadcasted_iota(jnp.int32, (lp, lp), 0)
    c_i = lax.broadcasted_iota(jnp.int32, (lp, lp), 1)
    tri = jnp.where(c_i <= r_i, 1.0, 0.0).astype(BF16)
    a_hi, a_mid, a_lo = _split3(ad)
    cs = _dot(tri, a_hi) + (_dot(tri, a_mid) + _dot(tri, a_lo))
    cs2 = cs * LOG2_E
    cs_t = cs2.T
    dt_t = dt_p.T
    cs_last = cs[lp - 1:lp, :]
    to_end = jnp.exp(cs_last - cs)
    w_t = (dt_p * to_end).T
    e_last_t = jnp.broadcast_to(jnp.exp(cs_last), (lp, LANES)).T
    cs_q = cs2[:rows, :]
    lane_q = lax.broadcasted_iota(jnp.int32, (rows, LANES), 1)
    causal = lax.broadcasted_iota(jnp.int32, (rows, lp), 1) <= lax.broadcasted_iota(jnp.int32, (rows, lp), 0)
    lane_lo = lane_q < SSM_HEAD_DIM

    y_groups = []
    for g in range(SSM_GROUPS):
        cg = cm[:, g * SSM_STATE:(g + 1) * SSM_STATE].astype(BF16)
        bg = _pad_rows(bm[:, g * SSM_STATE:(g + 1) * SSM_STATE], lp).astype(BF16)
        gw = SSM_HPG * SSM_HEAD_DIM
        hg = h_ref[g * gw:(g + 1) * gw, :]
        cb = _dot_nt(cg, bg)
        yoff = _dot_nt(cg, hg.astype(BF16))
        xs_g = _pad_rows(xs[:, g * gw:(g + 1) * gw], lp)
        y_pairs = []
        for j in range(SSM_HPG // 2):
            ha = g * SSM_HPG + 2 * j
            hb = ha + 1
            col_a = jnp.sum(jnp.where(lane_q == ha, cs_q, 0.0), axis=1, keepdims=True)
            col_b = jnp.sum(jnp.where(lane_q == hb, cs_q, 0.0), axis=1, keepdims=True)
            m_a = cb * jnp.exp2(jnp.where(causal, col_a - cs_t[ha:ha + 1, :], -jnp.inf)) * dt_t[ha:ha + 1, :]
            m_b = cb * jnp.exp2(jnp.where(causal, col_b - cs_t[hb:hb + 1, :], -jnp.inf)) * dt_t[hb:hb + 1, :]
            xs_pair = xs_g[:, j * LANES:(j + 1) * LANES].astype(BF16)
            ydiag = jnp.where(lane_lo, _dot(m_a.astype(BF16), xs_pair), _dot(m_b.astype(BF16), xs_pair))
            scale = jnp.where(lane_lo, jnp.exp2(col_a), jnp.exp2(col_b))
            y_pairs.append(ydiag + yoff[:, j * LANES:(j + 1) * LANES] * scale)
        y_groups.append(jnp.concatenate(y_pairs, axis=1))
        xs_t = xs_g.T
        w_rows = jnp.concatenate(
            [jnp.broadcast_to(w_t[g * SSM_HPG + j:g * SSM_HPG + j + 1, :], (SSM_HEAD_DIM, lp)) for j in range(SSM_HPG)],
            axis=0)
        d_rows = jnp.concatenate(
            [jnp.broadcast_to(e_last_t[g * SSM_HPG + j:g * SSM_HPG + j + 1, :], (SSM_HEAD_DIM, LANES))
             for j in range(SSM_HPG)], axis=0)
        st = _dot((xs_t * w_rows).astype(BF16), bg)
        h_ref[g * gw:(g + 1) * gw, :] = hg * d_rows + st

    y = jnp.concatenate(y_groups, axis=1)
    y = y + xs * dexp_ref[...]
    y = y * _silu(z_ref[...])
    gsz = SSM_INNER // SSM_GROUPS
    outs = []
    for g in range(SSM_GROUPS):
        yg = y[:, g * gsz:(g + 1) * gsz]
        ms = jnp.mean(yg * yg, axis=-1, keepdims=True)
        outs.append(yg * lax.rsqrt(ms + RMS_EPS))
    act_ref[...] = (jnp.concatenate(outs, axis=1) * nw_ref[...]).astype(act_ref.dtype)

    @pl.when(c == n_chunks - 1)
    def _():
        ht_ref[...] = h_ref[...]


def _ssd_core(z, xbc, dt_raw, carry0, h0, conv_w, conv_b, dt_bias, a_log, d_exp, norm_w, ht_buf, *, nseq, rows,
              n_chunks, t_valid, seq0, n_slots):
    aliased = ht_buf is not None
    n_fill = 0 if aliased else n_slots - seq0 - nseq
    kern = functools.partial(_ssd_kernel, nseq=nseq, n_fill=n_fill, aliased=aliased, rows=rows, n_chunks=n_chunks,
                             t_valid=t_valid)
    last_blk = nseq * n_chunks - 1
    rmap = lambda b, c: (jnp.where(b < nseq, b * n_chunks + c, last_blk), 0)
    smap = lambda b, c: (jnp.minimum(b, nseq - 1), 0, 0)
    const = lambda b, c: (0, 0)
    in_specs = [
        pl.BlockSpec((rows, SSM_INNER), rmap),
        pl.BlockSpec((rows, SSM_CONV_DIM), rmap),
        pl.BlockSpec((rows, LANES), rmap),
        pl.BlockSpec((None, SUBLANES, SSM_CONV_DIM), smap),
        pl.BlockSpec((None, SSM_INNER, SSM_STATE), lambda b, c: (seq0 + jnp.minimum(b, nseq - 1), 0, 0)),
        pl.BlockSpec((SSM_CONV, SSM_CONV_DIM), const),
        pl.BlockSpec((1, SSM_CONV_DIM), const),
        pl.BlockSpec((1, LANES), const),
        pl.BlockSpec((1, LANES), const),
        pl.BlockSpec((1, SSM_INNER), const),
        pl.BlockSpec((1, SSM_INNER), const),
    ]
    args = [z, xbc, dt_raw, carry0, h0, conv_w, conv_b, dt_bias, a_log, d_exp, norm_w]
    if aliased:
        in_specs.append(pl.BlockSpec(memory_space=pl.ANY))
        args.append(ht_buf)
    return pl.pallas_call(
        kern,
        grid=(nseq + n_fill, n_chunks),
        in_specs=in_specs,
        out_specs=[
            pl.BlockSpec((rows, SSM_INNER), rmap),
            pl.BlockSpec((None, SSM_INNER, SSM_STATE), lambda b, c: (seq0 + b, 0, 0)),
        ],
        out_shape=[
            jax.ShapeDtypeStruct((nseq * n_chunks * rows, SSM_INNER), BF16),
            jax.ShapeDtypeStruct((n_slots, SSM_INNER, SSM_STATE), F32),
        ],
        scratch_shapes=[
            pltpu.VMEM((SUBLANES, SSM_CONV_DIM), F32),
            pltpu.VMEM((SSM_INNER, SSM_STATE), F32),
        ],
        input_output_aliases={11: 1} if aliased else {},
        compiler_params=_cparams("arbitrary", "arbitrary"),
        name="ssd_core",
    )(*args)


def _lru_kernel(gin_ref, xr_ref, carry0_ref, h0_ref, cw_ref, cb_ref, wa_ref, ba_ref, wi_ref, bi_ref, lam_ref,
                act_ref, ht_ref, xp_ref, hc_ref, *, rows, n_chunks, t_valid):
    c = pl.program_id(1)

    @pl.when(c == 0)
    def _():
        hc_ref[...] = h0_ref[...]

    xc = _causal_conv(xp_ref, xr_ref[...], cw_ref, cb_ref[...], carry0_ref, LRU_CONV, rows, c == 0)
    ra = []
    ia = []
    for h in range(LRU_HEADS):
        xh = xc[:, h * LRU_BLOCK:(h + 1) * LRU_BLOCK].astype(BF16)
        ra.append(_dot(xh, wa_ref[h].astype(BF16)))
        ia.append(_dot(xh, wi_ref[h].astype(BF16)))
    r = jax.nn.sigmoid(jnp.concatenate(ra, axis=1) + ba_ref[...])
    ig = jax.nn.sigmoid(jnp.concatenate(ia, axis=1) + bi_ref[...])
    log_a = -LRU_C * r * _softplus(-lam_ref[...])
    a = jnp.exp(log_a)
    th = jnp.tanh(log_a)
    mult = jnp.sqrt(-2.0 * th / (1.0 - th))
    u = mult * (ig * xc)
    if t_valid < rows:
        row = lax.broadcasted_iota(jnp.int32, (rows, LRU_WIDTH), 0)
        a = jnp.where(row < t_valid, a, 1.0)
        u = jnp.where(row < t_valid, u, 0.0)

    sub = lax.broadcasted_iota(jnp.int32, (SUBLANES, LRU_WIDTH), 0)
    h_prev = hc_ref[...]
    h_blocks = []
    for blk in range(rows // SUBLANES):
        ab = a[blk * SUBLANES:(blk + 1) * SUBLANES, :]
        ub = u[blk * SUBLANES:(blk + 1) * SUBLANES, :]
        for d in (1, 2, 4):
            a_sh = jnp.where(sub >= d, pltpu.roll(ab, d, 0), 1.0)
            u_sh = jnp.where(sub >= d, pltpu.roll(ub, d, 0), 0.0)
            ub = ub + ab * u_sh
            ab = ab * a_sh
        hb = ub + ab * h_prev
        h_blocks.append(hb)
        h_prev = hb[SUBLANES - 1:SUBLANES, :]
    hc_ref[...] = h_prev
    hseq = jnp.concatenate(h_blocks, axis=0)
    gx = gin_ref[...]
    gate = 0.5 * gx * (1.0 + jnp.tanh(math.sqrt(2.0 / math.pi) * (gx + 0.044715 * (gx * gx * gx))))
    act_ref[...] = (hseq * gate).astype(act_ref.dtype)

    @pl.when(c == n_chunks - 1)
    def _():
        ht_ref[...] = hc_ref[...]


def _lru_core(proj, carry0, h0, conv_w, conv_b, wa, ba, wi, bi, lam, *, nseq, rows, n_chunks, t_valid):
    kern = functools.partial(_lru_kernel, rows=rows, n_chunks=n_chunks, t_valid=t_valid)
    w = LRU_WIDTH
    const = lambda b, c: (0, 0)
    const3 = lambda b, c: (0, 0, 0)
    return pl.pallas_call(
        kern,
        grid=(nseq, n_chunks),
        in_specs=[
            pl.BlockSpec((rows, w), lambda b, c: (b * n_chunks + c, 0)),
            pl.BlockSpec((rows, w), lambda b, c: (b * n_chunks + c, 1)),
            pl.BlockSpec((None, SUBLANES, w), lambda b, c: (b, 0, 0)),
            pl.BlockSpec((None, 1, w), lambda b, c: (b, 0, 0)),
            pl.BlockSpec((LRU_CONV, w), const),
            pl.BlockSpec((1, w), const),
            pl.BlockSpec((LRU_HEADS, LRU_BLOCK, LRU_BLOCK), const3),
            pl.BlockSpec((1, w), const),
            pl.BlockSpec((LRU_HEADS, LRU_BLOCK, LRU_BLOCK), const3),
            pl.BlockSpec((1, w), const),
            pl.BlockSpec((1, w), const),
        ],
        out_specs=[
            pl.BlockSpec((rows, w), lambda b, c: (b * n_chunks + c, 0)),
            pl.BlockSpec((None, 1, w), lambda b, c: (b, 0, 0)),
        ],
        out_shape=[
            jax.ShapeDtypeStruct((nseq * n_chunks * rows, w), BF16),
            jax.ShapeDtypeStruct((nseq, 1, w), F32),
        ],
        scratch_shapes=[pltpu.VMEM((SUBLANES, w), F32), pltpu.VMEM((1, w), F32)],
        compiler_params=_cparams("parallel", "arbitrary"),
        name="lru_core",
    )(proj, proj, carry0, h0, conv_w, conv_b, wa, ba, wi, bi, lam)


def _sc_kernel(bg_ref, cg_ref, hh_ref, carry0_ref, cw_ref, act_ref, vt_ref, xp_ref, *, rows, n_chunks, t_valid):
    c = pl.program_id(1)
    v = cg_ref[...] * hh_ref[...]
    u = _causal_conv(xp_ref, v, cw_ref, None, carry0_ref, SC_WIDTH, rows, c == 0)
    act_ref[...] = (bg_ref[...] * u).astype(act_ref.dtype)
    lo = max(t_valid - SUBLANES, 0)

    @pl.when(c == n_chunks - 1)
    def _():
        vt_ref[...] = v[lo:lo + SUBLANES, :]


def _sc_core(proj, carry0, conv_w, *, nseq, rows, n_chunks, t_valid):
    kern = functools.partial(_sc_kernel, rows=rows, n_chunks=n_chunks, t_valid=t_valid)
    d = D_MODEL
    return pl.pallas_call(
        kern,
        grid=(nseq, n_chunks),
        in_specs=[
            pl.BlockSpec((rows, d), lambda b, c: (b * n_chunks + c, 0)),
            pl.BlockSpec((rows, d), lambda b, c: (b * n_chunks + c, 1)),
            pl.BlockSpec((rows, d), lambda b, c: (b * n_chunks + c, 2)),
            pl.BlockSpec((None, SUBLANES, d), lambda b, c: (b, 0, 0)),
            pl.BlockSpec((SC_WIDTH, d), lambda b, c: (0, 0)),
        ],
        out_specs=[
            pl.BlockSpec((rows, d), lambda b, c: (b * n_chunks + c, 0)),
            pl.BlockSpec((None, SUBLANES, d), lambda b, c: (b, 0, 0)),
        ],
        out_shape=[
            jax.ShapeDtypeStruct((nseq * n_chunks * rows, d), BF16),
            jax.ShapeDtypeStruct((nseq, SUBLANES, d), F32),
        ],
        scratch_shapes=[pltpu.VMEM((SUBLANES, d), F32)],
        compiler_params=_cparams("parallel", "arbitrary"),
        name="sconv_core",
    )(proj, proj, proj, carry0, conv_w)


def _route_kernel(lg_ref, info_ref, w12_ref, cnt_ref, carry_ref, *, tm, n_tiles):
    i = pl.program_id(0)

    @pl.when(i == 0)
    def _():
        carry_ref[...] = jnp.zeros_like(carry_ref)

    lg = lg_ref[...]
    lane = lax.broadcasted_iota(jnp.int32, (tm, LANES), 1).astype(F32)
    neg = -jnp.inf
    big = float(LANES)
    is_g = lane < MOE_GROUPS
    gl = jnp.where(is_g, lg, neg)
    gmax = jnp.max(gl, axis=1, keepdims=True)
    gidx = jnp.min(jnp.where(gl == gmax, lane, big), axis=1, keepdims=True)
    gsum = jnp.sum(jnp.where(is_g, jnp.exp(lg - gmax), 0.0), axis=1, keepdims=True)
    gw = 1.0 / gsum
    rel = lane - (MOE_GROUPS + MOE_PER_GROUP * gidx)
    in_grp = (rel >= 0.0) & (rel < MOE_PER_GROUP)
    el = jnp.where(in_grp, lg, neg)
    v1 = jnp.max(el, axis=1, keepdims=True)
    i1 = jnp.min(jnp.where(in_grp & (el == v1), lane, big), axis=1, keepdims=True)
    in_grp2 = in_grp & (lane != i1)
    el2 = jnp.where(in_grp2, lg, neg)
    v2 = jnp.max(el2, axis=1, keepdims=True)
    i2 = jnp.min(jnp.where(in_grp2 & (el2 == v2), lane, big), axis=1, keepdims=True)
    t = jnp.exp(v2 - v1)
    den = 1.0 + t
    w1 = (1.0 / den) * gw
    w2 = (t / den) * gw
    e1 = i1 - MOE_GROUPS
    e2 = i2 - MOE_GROUPS
    onehot = jnp.where(lane == e1, 1.0, jnp.where(lane == e2, 1.0, 0.0))
    r_i = lax.broadcasted_iota(jnp.int32, (tm, tm), 0)
    c_i = lax.broadcasted_iota(jnp.int32, (tm, tm), 1)
    tri = jnp.where(c_i < r_i, 1.0, 0.0).astype(BF16)
    before = _dot(tri, onehot.astype(BF16)) + carry_ref[0:1, :]
    r1 = jnp.sum(jnp.where(lane == e1, before, 0.0), axis=1, keepdims=True)
    r2 = jnp.sum(jnp.where(lane == e2, before, 0.0), axis=1, keepdims=True)
    carry_ref[0:1, :] = carry_ref[0:1, :] + jnp.sum(onehot, axis=0, keepdims=True)
    info_ref[...] = jnp.where(lane == 0, e1, jnp.where(lane == 1, e2, jnp.where(lane == 2, r1, jnp.where(
        lane == 3, r2, 0.0))))
    w12_ref[...] = jnp.concatenate(
        [jnp.broadcast_to(w1, (tm, LANES)), jnp.broadcast_to(w2, (tm, LANES))], axis=1)

    @pl.when(i == n_tiles - 1)
    def _():
        cnt_ref[...] = carry_ref[...]


def _route(logits, tm):
    n = logits.shape[0]
    n_tiles = n // tm
    kern = functools.partial(_route_kernel, tm=tm, n_tiles=n_tiles)
    return pl.pallas_call(
        kern,
        grid=(n_tiles,),
        in_specs=[pl.BlockSpec((tm, LANES), lambda i: (i, 0))],
        out_specs=[
            pl.BlockSpec((tm, LANES), lambda i: (i, 0)),
            pl.BlockSpec((tm, 2 * LANES), lambda i: (i, 0)),
            pl.BlockSpec((SUBLANES, LANES), lambda i: (0, 0)),
        ],
        out_shape=[
            jax.ShapeDtypeStruct((n, LANES), F32),
            jax.ShapeDtypeStruct((n, 2 * LANES), F32),
            jax.ShapeDtypeStruct((SUBLANES, LANES), F32),
        ],
        scratch_shapes=[pltpu.VMEM((SUBLANES, LANES), F32)],
        compiler_params=_cparams("arbitrary"),
        name="route",
    )(logits)


def _dispatch_kernel(d0_ref, d1_ref, x_ref, nw_ref, xg_ref, xn_ref, sem, *, tm):
    base = pl.program_id(0) * tm
    x = x_ref[...]
    ms = jnp.mean(x * x, axis=-1, keepdims=True)
    xn_ref[...] = x * lax.rsqrt(ms + RMS_EPS) * nw_ref[...]

    def row_copy(r, d):
        return pltpu.make_async_copy(xn_ref.at[pl.ds(r, 1), :], xg_ref.at[pl.ds(d, 1), :], sem)

    def issue(g, carry):
        r0 = pl.multiple_of(g * SUBLANES, SUBLANES)
        for j in range(SUBLANES):
            row_copy(r0 + j, d0_ref[base + r0 + j]).start(priority=0)
            row_copy(r0 + j, d1_ref[base + r0 + j]).start(priority=1)
        return carry

    lax.fori_loop(0, tm // SUBLANES, issue, 0)

    def drain(g, carry):
        for _ in range(2 * SUBLANES):
            row_copy(0, 0).wait()
        return carry

    lax.fori_loop(0, tm // SUBLANES, drain, 0)


def _dispatch(x1, norm_w, dest0, dest1, tm):
    n, d = x1.shape
    grid_spec = pltpu.PrefetchScalarGridSpec(
        num_scalar_prefetch=2,
        grid=(n // tm,),
        in_specs=[pl.BlockSpec((tm, d), lambda i, *_: (i, 0)), pl.BlockSpec((1, d), lambda i, *_: (0, 0))],
        out_specs=pl.BlockSpec(memory_space=pl.ANY),
        scratch_shapes=[pltpu.VMEM((tm, d), F32), pltpu.SemaphoreType.DMA(())],
    )
    return pl.pallas_call(
        functools.partial(_dispatch_kernel, tm=tm),
        grid_spec=grid_spec,
        out_shape=jax.ShapeDtypeStruct((2 * n, d), F32),
        compiler_params=_cparams("arbitrary"),
        name="moe_dispatch",
    )(dest0, dest1, x1, norm_w.reshape(1, d))


VISIT_VALID = 1
VISIT_FIRST_OF_EXPERT = 2
VISIT_SLOT = 4


def _ffn_kernel(vt_ref, ve_ref, vf_ref, vn_ref, off_ref, x_ref, wg_hbm, wu_hbm, wd_hbm, o_ref,
                sg_ref, su_ref, sd_ref, cg_ref, cu_ref, cd_ref, sem, *, tm, layer):
    v = pl.program_id(0)
    t = vt_ref[v]
    e = ve_ref[v]
    flags = vf_ref[v]
    first_of_tile = jnp.logical_or(v == 0, vt_ref[jnp.maximum(v - 1, 0)] != t)

    @pl.when(first_of_tile)
    def _():
        o_ref[...] = jnp.zeros_like(o_ref)

    def copies(expert, slot):
        return (pltpu.make_async_copy(wg_hbm.at[layer, expert], sg_ref.at[slot], sem.at[slot, 0]),
                pltpu.make_async_copy(wu_hbm.at[layer, expert], su_ref.at[slot], sem.at[slot, 1]),
                pltpu.make_async_copy(wd_hbm.at[layer, expert], sd_ref.at[slot], sem.at[slot, 2]))

    @pl.when((flags & VISIT_FIRST_OF_EXPERT) != 0)
    def _():
        slot = (flags & VISIT_SLOT) // VISIT_SLOT
        nxt = vn_ref[v]

        @pl.when(v == 0)
        def _():
            for c in copies(e, slot):
                c.start()

        @pl.when(nxt >= 0)
        def _():
            for c in copies(nxt, 1 - slot):
                c.start()

        for c in copies(e, slot):
            c.wait()
        cg_ref[...] = sg_ref[slot].astype(BF16)
        cu_ref[...] = su_ref[slot].astype(BF16)
        cd_ref[...] = sd_ref[slot].astype(BF16)

    @pl.when((flags & VISIT_VALID) != 0)
    def _():
        row = t * tm + lax.broadcasted_iota(jnp.int32, (tm, MOE_FF), 0)
        mine = (row >= off_ref[e]) & (row < off_ref[e + 1])
        x = x_ref[...].astype(BF16)
        hg = _dot(x, cg_ref[...])
        hu = _dot(x, cu_ref[...])
        hid = jnp.where(mine, _silu(hg) * hu, 0.0)
        o_ref[...] += _dot(hid.astype(BF16), cd_ref[...])


def _ffn(xg, w_gate, w_up, w_down, layer, plan, tm):
    rows, d = xg.shape
    n_visits = plan[0].shape[0]
    kern = functools.partial(_ffn_kernel, tm=tm, layer=layer)
    grid_spec = pltpu.PrefetchScalarGridSpec(
        num_scalar_prefetch=5,
        grid=(n_visits,),
        in_specs=[
            pl.BlockSpec((tm, d), lambda v, vt, *_: (vt[v], 0)),
            pl.BlockSpec(memory_space=pl.ANY),
            pl.BlockSpec(memory_space=pl.ANY),
            pl.BlockSpec(memory_space=pl.ANY),
        ],
        out_specs=pl.BlockSpec((tm, d), lambda v, vt, *_: (vt[v], 0)),
        scratch_shapes=[
            pltpu.VMEM((2, d, MOE_FF), F32), pltpu.VMEM((2, d, MOE_FF), F32), pltpu.VMEM((2, MOE_FF, d), F32),
            pltpu.VMEM((d, MOE_FF), BF16), pltpu.VMEM((d, MOE_FF), BF16), pltpu.VMEM((MOE_FF, d), BF16),
            pltpu.SemaphoreType.DMA((2, 3)),
        ],
    )
    return pl.pallas_call(
        kern,
        grid_spec=grid_spec,
        out_shape=jax.ShapeDtypeStruct((rows, d), F32),
        compiler_params=_cparams("arbitrary"),
        name="moe_ffn",
    )(*plan, xg, w_gate, w_up, w_down)


def _visit_plan(counts, n_tiles, tm):
    off = jnp.concatenate([jnp.zeros((1,), jnp.int32), jnp.cumsum(counts)])
    n_visits = n_tiles + N_EXPERTS - 1
    first_tile = off[:-1] // tm
    last_tile = jnp.where(counts > 0, (off[1:] - 1) // tm, first_tile)
    n_e = jnp.where(counts > 0, last_tile - first_tile + 1, 0)
    v_off = jnp.concatenate([jnp.zeros((1,), jnp.int32), jnp.cumsum(n_e)])
    total = v_off[-1]
    vid = jnp.arange(n_visits, dtype=jnp.int32)
    exp_of = jnp.sum((vid[:, None] >= v_off[None, 1:]).astype(jnp.int32), axis=1)
    exp_of = jnp.minimum(exp_of, N_EXPERTS - 1)
    tile_of = first_tile[exp_of] + (vid - v_off[exp_of])
    valid = vid < total
    first_of_expert = valid & (vid == v_off[exp_of])
    used = counts > 0
    order = jnp.cumsum(used.astype(jnp.int32)) - 1
    ids = jnp.arange(N_EXPERTS, dtype=jnp.int32)
    later = jnp.where(used[None, :] & (ids[None, :] > ids[:, None]), ids[None, :], N_EXPERTS)
    next_used = jnp.min(later, axis=1)
    next_used = jnp.where(next_used < N_EXPERTS, next_used, -1)
    last_e = jnp.minimum(jnp.sum((total - 1 >= v_off[1:]).astype(jnp.int32)), N_EXPERTS - 1)
    exp_of = jnp.where(valid, exp_of, last_e)
    tile_of = jnp.where(valid, tile_of, n_tiles - 1)
    flags = valid * VISIT_VALID + first_of_expert * VISIT_FIRST_OF_EXPERT + (order[exp_of] % 2) * VISIT_SLOT
    i32 = lambda a: a.astype(jnp.int32)
    return i32(tile_of), i32(exp_of), i32(flags), i32(next_used[exp_of]), i32(off)


def _combine_kernel(d0_ref, d1_ref, x1_ref, w12_ref, nw_ref, yg_ref, *refs, tm, n_tiles, n_prompt_tiles, final):
    outs, (buf, sem) = refs[:-2], refs[-2:]
    i = pl.program_id(0)

    def row_copy(slot, k, r, d):
        return pltpu.make_async_copy(yg_ref.at[pl.ds(d, 1), :], buf.at[slot, k, pl.ds(r, 1), :], sem.at[slot])

    def issue(tile, slot):
        base = tile * tm

        def body(g, carry):
            r0 = pl.multiple_of(g * SUBLANES, SUBLANES)
            for j in range(SUBLANES):
                row_copy(slot, 0, r0 + j, d0_ref[base + r0 + j]).start(priority=0)
                row_copy(slot, 1, r0 + j, d1_ref[base + r0 + j]).start(priority=1)
            return carry

        lax.fori_loop(0, tm // SUBLANES, body, 0)

    @pl.when(i == 0)
    def _():
        issue(0, 0)

    @pl.when(i + 1 < n_tiles)
    def _():
        issue(i + 1, lax.rem(i + 1, 2))

    slot = lax.rem(i, 2)

    def drain(g, carry):
        for _ in range(SUBLANES):
            row_copy(slot, 0, 0, 0).wait()
            row_copy(slot, 1, 0, 0).wait()
        return carry

    lax.fori_loop(0, tm // SUBLANES, drain, 0)

    w12 = w12_ref[...]
    reps = x1_ref.shape[1] // LANES
    w1 = jnp.concatenate([w12[:, :LANES]] * reps, axis=1)
    w2 = jnp.concatenate([w12[:, LANES:]] * reps, axis=1)
    x2 = x1_ref[...] + (w1 * buf[slot, 0] + w2 * buf[slot, 1])
    ms = jnp.mean(x2 * x2, axis=-1, keepdims=True)
    xn = x2 * lax.rsqrt(ms + RMS_EPS) * nw_ref[...]
    if final:
        split, split_val = outs, xn
    else:
        split, split_val = outs[:2], x2
        outs[2][...] = xn.astype(outs[2].dtype)

    @pl.when(i < n_prompt_tiles)
    def _():
        split[0][...] = split_val

    @pl.when(i >= n_prompt_tiles)
    def _():
        split[1][...] = split_val


def _combine(x1, w12, yg, dest0, dest1, norm_w, n_p, tm, final):
    n, d = x1.shape
    n_tiles = n // tm
    npt = n_p // tm
    spec_p = pl.BlockSpec((tm, d), lambda i, *_: (jnp.minimum(i, npt - 1), 0))
    spec_s = pl.BlockSpec((tm, d), lambda i, *_: (jnp.maximum(i - npt, 0), 0))
    out_specs = [spec_p, spec_s]
    out_shape = [jax.ShapeDtypeStruct((n_p, d), F32), jax.ShapeDtypeStruct((n - n_p, d), F32)]
    if not final:
        out_specs.append(pl.BlockSpec((tm, d), lambda i, *_: (i, 0)))
        out_shape.append(jax.ShapeDtypeStruct((n, d), BF16))
    grid_spec = pltpu.PrefetchScalarGridSpec(
        num_scalar_prefetch=2,
        grid=(n_tiles,),
        in_specs=[
            pl.BlockSpec((tm, d), lambda i, *_: (i, 0)),
            pl.BlockSpec((tm, 2 * LANES), lambda i, *_: (i, 0)),
            pl.BlockSpec((1, d), lambda i, *_: (0, 0)),
            pl.BlockSpec(memory_space=pl.ANY),
        ],
        out_specs=out_specs,
        scratch_shapes=[pltpu.VMEM((2, 2, tm, d), F32), pltpu.SemaphoreType.DMA((2,))],
    )
    return pl.pallas_call(
        functools.partial(_combine_kernel, tm=tm, n_tiles=n_tiles, n_prompt_tiles=npt, final=final),
        grid_spec=grid_spec,
        out_shape=out_shape,
        compiler_params=_cparams("arbitrary"),
        name="moe_combine",
    )(dest0, dest1, x1, w12, norm_w.reshape(1, d), yg)


def _pad_lanes(v, width=LANES):
    v = v.reshape(1, -1)
    return jnp.pad(v, ((0, 0), (0, width - v.shape[1])))


def _sample_rows(m, n_p, b_s, t_s):
    c = m.shape[1]
    s = m[n_p:].reshape(b_s, t_s, c)
    s = jnp.pad(s, ((0, 0), (0, SAMPLE_PAD_T - t_s), (0, 0)))
    return s.reshape(b_s * SAMPLE_PAD_T, c)


def _sample_act(act, b_s, t_s):
    c = act.shape[1]
    return act.reshape(b_s, SAMPLE_PAD_T, c)[:, :t_s].reshape(b_s * t_s, c)


def _carry_rows(state, width):
    return jnp.pad(state, ((0, 0), (SUBLANES - (width - 1), 0), (0, 0)))


def _moe(x1, ffn_norm_w, logits, layer, w_gate, w_up, w_down, next_norm_w, n_p, final):
    n = x1.shape[0]
    info, w12, cnt = _route(logits, ROUTE_TM)
    counts = cnt[0, :N_EXPERTS].astype(jnp.int32)
    er = info[:, 0:4].T.astype(jnp.int32)
    plan = _visit_plan(counts, (2 * n) // FFN_TM, FFN_TM)
    off = plan[-1]
    dest0 = off[er[0]] + er[2]
    dest1 = off[er[1]] + er[3]
    xg = _dispatch(x1, ffn_norm_w, dest0, dest1, DISPATCH_TM)
    yg = _ffn(xg, w_gate, w_up, w_down, layer, plan, FFN_TM)
    return _combine(x1, w12, yg, dest0, dest1, next_norm_w, n_p, COMBINE_TM, final)


def kernel(x_prompt, x_sample, state_ssm_conv, state_ssm, state_lru_conv, state_lru, state_sconv, norm_mix, norm_ffn, norm_final, w_ssm_in, ssm_conv_w, ssm_conv_b, ssm_dt_bias, ssm_A_log, ssm_D, ssm_norm_w, w_ssm_out, w_lru_in, lru_conv_w, lru_conv_b, lru_wa, lru_ba, lru_wi, lru_bi, lru_lambda, w_lru_out, w_sc_in, sc_conv_w, w_sc_out, w_route_group, w_route_expert, w_exp_gate, w_exp_up, w_exp_down):
    b_p, t_p, d = x_prompt.shape
    b_s, t_s, _ = x_sample.shape
    n_p = b_p * t_p
    n_s = b_s * t_s
    n = n_p + n_s
    assert n_s == ROW_TM and n_p % ROW_TM == 0 and t_p % SSM_CHUNK == 0 and t_s <= SAMPLE_PAD_T
    mm_tm = n // 4
    n_chunks_p = t_p // SSM_CHUNK
    n_a = state_ssm.shape[0]

    x_p = x_prompt.reshape(n_p, d)
    x_s = x_sample.reshape(n_s, d)
    xn = _rmsnorm(x_p, x_s, norm_mix[0], ROW_TM, BF16)

    def last_rows_p(m, col0, ncol, k):
        return jnp.stack([m[(b + 1) * t_p - k:(b + 1) * t_p, col0:col0 + ncol] for b in range(b_p)])

    def last_rows_s(state, m, col0, ncol):
        new = m[n_p:, col0:col0 + ncol].reshape(b_s, t_s, ncol)
        return jnp.concatenate([state, new], axis=1)[:, t_s:]

    ssd_h_p = ssd_h_s = None
    ssd_h0_p = jnp.zeros((n_a * b_p, SSM_INNER, SSM_STATE), F32)
    ssd_h0_s = state_ssm.reshape(n_a * b_s, SSM_INNER, SSM_STATE)
    w_ssm_in_t = jnp.swapaxes(w_ssm_in, 1, 2)

    p_ssm_conv, p_lru_conv, p_lru, p_sconv = [], [], [], []
    s_ssm_conv, s_lru_conv, s_lru, s_sconv = [], [], [], []

    for i in range(DEPTH):
        kind = i % N_MIXERS
        s = i // N_MIXERS
        w_r = jnp.pad(jnp.concatenate([w_route_group[i], w_route_expert[i]], axis=1),
                      ((0, 0), (0, LANES - MOE_GROUPS - N_EXPERTS)))
        r_hi = w_r.astype(BF16)
        r_lo = (w_r - r_hi.astype(F32)).astype(BF16)
        if kind == 0:
            z = _matmul(xn, w_ssm_in_t, s, 0, SSM_INNER, PROJ_TN, mm_tm, transposed=True)
            xbc = _matmul(xn, w_ssm_in_t, s, SSM_INNER // PROJ_TN, SSM_CONV_DIM, PROJ_TN, mm_tm, transposed=True)
            w_dt = jnp.pad(w_ssm_in_t[s, SSM_INNER + SSM_CONV_DIM:, :], ((0, LANES - SSM_HEADS), (0, 0)))[None]
            dt_raw = _matmul(xn, w_dt, 0, 0, LANES, LANES, mm_tm, transposed=True)
            params = (ssm_conv_w[s], ssm_conv_b[s].reshape(1, -1), _pad_lanes(ssm_dt_bias[s]),
                      _pad_lanes(ssm_A_log[s]), jnp.repeat(ssm_D[s], SSM_HEAD_DIM).reshape(1, -1),
                      ssm_norm_w[s].reshape(1, -1))
            act_p, ssd_h_p = _ssd_core(
                z, xbc, dt_raw, jnp.zeros((b_p, SUBLANES, SSM_CONV_DIM), F32), ssd_h0_p, *params, ssd_h_p,
                nseq=b_p, rows=SSM_CHUNK, n_chunks=n_chunks_p, t_valid=SSM_CHUNK, seq0=s * b_p, n_slots=n_a * b_p)
            act_s, ssd_h_s = _ssd_core(
                _sample_rows(z, n_p, b_s, t_s), _sample_rows(xbc, n_p, b_s, t_s), _sample_rows(dt_raw, n_p, b_s, t_s),
                _carry_rows(state_ssm_conv[s], SSM_CONV), ssd_h0_s, *params, ssd_h_s,
                nseq=b_s, rows=SAMPLE_PAD_T, n_chunks=1, t_valid=t_s, seq0=s * b_s, n_slots=n_a * b_s)
            act_s = _sample_act(act_s, b_s, t_s)
            p_ssm_conv.append(last_rows_p(xbc, 0, SSM_CONV_DIM, SSM_CONV - 1))
            s_ssm_conv.append(last_rows_s(state_ssm_conv[s], xbc, 0, SSM_CONV_DIM))
            w_out = w_ssm_out
        elif kind == 1:
            proj = _matmul(xn, w_lru_in, s, 0, 2 * LRU_WIDTH, PROJ_TN, mm_tm)
            params = (lru_conv_w[s], lru_conv_b[s].reshape(1, -1), lru_wa[s], lru_ba[s].reshape(1, -1), lru_wi[s],
                      lru_bi[s].reshape(1, -1), lru_lambda[s].reshape(1, -1))
            act_p, h_p = _lru_core(
                proj, jnp.zeros((b_p, SUBLANES, LRU_WIDTH), F32), jnp.zeros((b_p, 1, LRU_WIDTH), F32), *params,
                nseq=b_p, rows=SSM_CHUNK, n_chunks=n_chunks_p, t_valid=SSM_CHUNK)
            act_s, h_s = _lru_core(
                _sample_rows(proj, n_p, b_s, t_s), _carry_rows(state_lru_conv[s], LRU_CONV),
                state_lru[s].reshape(b_s, 1, LRU_WIDTH), *params,
                nseq=b_s, rows=SAMPLE_PAD_T, n_chunks=1, t_valid=t_s)
            act_s = _sample_act(act_s, b_s, t_s)
            p_lru_conv.append(last_rows_p(proj, LRU_WIDTH, LRU_WIDTH, LRU_CONV - 1))
            s_lru_conv.append(last_rows_s(state_lru_conv[s], proj, LRU_WIDTH, LRU_WIDTH))
            p_lru.append(h_p.reshape(b_p, LRU_WIDTH))
            s_lru.append(h_s.reshape(b_s, LRU_WIDTH))
            w_out = w_lru_out
        else:
            proj = _matmul(xn, w_sc_in, s, 0, 3 * d, PROJ_TN, mm_tm)
            act_p, v_p = _sc_core(proj, jnp.zeros((b_p, SUBLANES, d), F32), sc_conv_w[s],
                                  nseq=b_p, rows=SSM_CHUNK, n_chunks=n_chunks_p, t_valid=SSM_CHUNK)
            act_s, v_s = _sc_core(_sample_rows(proj, n_p, b_s, t_s), _carry_rows(state_sconv[s], SC_WIDTH),
                                  sc_conv_w[s], nseq=b_s, rows=SAMPLE_PAD_T, n_chunks=1, t_valid=t_s)
            act_s = _sample_act(act_s, b_s, t_s)
            p_sconv.append(v_p[:, SUBLANES - (SC_WIDTH - 1):])
            s_sconv.append(jnp.concatenate([state_sconv[s], v_s[:, :t_s]], axis=1)[:, t_s:])
            w_out = w_sc_out

        x1, logits = _outproj(act_p, act_s, w_out[s].astype(BF16), x_p, x_s, norm_ffn[i], r_hi, r_lo, PROJ_TK)
        moe_w = (w_exp_gate, w_exp_up, w_exp_down)
        if i < DEPTH - 1:
            x_p, x_s, xn = _moe(x1, norm_ffn[i], logits, i, *moe_w, norm_mix[i + 1], n_p, False)
        else:
            y_p, y_s = _moe(x1, norm_ffn[i], logits, i, *moe_w, norm_final, n_p, True)

    state_shape = (SSM_HEADS, SSM_HEAD_DIM, SSM_STATE)
    return (y_p.reshape(b_p, t_p, d), y_s.reshape(b_s, t_s, d),
            jnp.stack(p_ssm_conv), ssd_h_p.reshape((n_a, b_p) + state_shape), jnp.stack(p_lru_conv), jnp.stack(p_lru),
            jnp.stack(p_sconv),
            jnp.stack(s_ssm_conv), ssd_h_s.reshape((n_a, b_s) + state_shape), jnp.stack(s_lru_conv), jnp.stack(s_lru),
            jnp.stack(s_sconv))
```

```python
import functools
import math

import jax
import jax.numpy as jnp
from jax import lax
from jax.experimental import pallas as pl
from jax.experimental.pallas import tpu as pltpu

F32 = jnp.float32
BF16 = jnp.bfloat16

D_MODEL = 2048
DEPTH = 4
N_MIXERS = 3
RMS_EPS = 1e-6
LOG2_E = 1.4426950408889634
SSM_INNER = 4096
SSM_HEAD_DIM = 64
SSM_HEADS = 64
SSM_GROUPS = 8
SSM_HPG = 8
SSM_STATE = 128
SSM_CONV = 4
SSM_BC = SSM_GROUPS * SSM_STATE
SSM_CONV_DIM = SSM_INNER + 2 * SSM_BC
SSM_CHUNK = 128
LRU_WIDTH = 2048
LRU_HEADS = 16
LRU_BLOCK = 128
LRU_CONV = 4
LRU_C = 8.0
SC_WIDTH = 3
MOE_GROUPS = 8
MOE_PER_GROUP = 4
N_EXPERTS = 32
MOE_FF = 512

LANES = 128
SUBLANES = 8
SAMPLE_PAD_T = 16
VMEM_LIMIT = 56 * 1024 * 1024

ROW_TM = 512
PROJ_TN = 512
PROJ_TK = 1024
ROUTE_TM = 512
DISPATCH_TM = 512
FFN_TM = 256
COMBINE_TM = 256


def _cparams(*sem):
    return pltpu.CompilerParams(dimension_semantics=sem, vmem_limit_bytes=VMEM_LIMIT)


def _silu(x):
    return x * jax.nn.sigmoid(x)


def _softplus(x):
    return jnp.maximum(x, 0.0) + jnp.log1p(jnp.exp(-jnp.abs(x)))


def _dot(a, b):
    return jnp.dot(a, b, preferred_element_type=F32)


def _dot_nt(a, b):
    return lax.dot_general(a, b, (((1,), (1,)), ((), ())), preferred_element_type=F32)


def _split3(x):
    hi = x.astype(BF16)
    r = x - hi.astype(F32)
    mid = r.astype(BF16)
    lo = (r - mid.astype(F32)).astype(BF16)
    return hi, mid, lo


def _norm_kernel(xp_ref, xs_ref, w_ref, o_ref, *, n_prompt_tiles):
    def emit(x):
        ms = jnp.mean(x * x, axis=-1, keepdims=True)
        o_ref[...] = (x * lax.rsqrt(ms + RMS_EPS) * w_ref[...]).astype(o_ref.dtype)

    @pl.when(pl.program_id(0) < n_prompt_tiles)
    def _():
        emit(xp_ref[...])

    @pl.when(pl.program_id(0) >= n_prompt_tiles)
    def _():
        emit(xs_ref[...])


def _rmsnorm(x_p, x_s, w, tm, out_dtype):
    n_p, d = x_p.shape
    n = n_p + x_s.shape[0]
    npt = n_p // tm
    return pl.pallas_call(
        functools.partial(_norm_kernel, n_prompt_tiles=npt),
        grid=(n // tm,),
        in_specs=[
            pl.BlockSpec((tm, d), lambda i: (jnp.minimum(i, npt - 1), 0)),
            pl.BlockSpec((tm, d), lambda i: (jnp.maximum(i - npt, 0), 0)),
            pl.BlockSpec((1, d), lambda i: (0, 0)),
        ],
        out_specs=pl.BlockSpec((tm, d), lambda i: (i, 0)),
        out_shape=jax.ShapeDtypeStruct((n, d), out_dtype),
        compiler_params=_cparams("arbitrary"),
        name="rmsnorm",
    )(x_p, x_s, w.reshape(1, d))


def _mm_kernel(a_ref, w_ref, o_ref, *, transposed):
    w = w_ref[...].astype(BF16)
    o_ref[...] = _dot_nt(a_ref[...], w) if transposed else _dot(a_ref[...], w)


def _matmul(a, w_stack, layer, col_blk0, ncols, tn, tm, transposed=False):
    n, k = a.shape
    if transposed:
        w_spec = pl.BlockSpec((None, tn, k), lambda i, j: (layer, col_blk0 + j, 0))
    else:
        w_spec = pl.BlockSpec((None, k, tn), lambda i, j: (layer, 0, col_blk0 + j))
    return pl.pallas_call(
        functools.partial(_mm_kernel, transposed=transposed),
        grid=(n // tm, ncols // tn),
        in_specs=[pl.BlockSpec((tm, k), lambda i, j: (i, 0)), w_spec],
        out_specs=pl.BlockSpec((tm, tn), lambda i, j: (i, j)),
        out_shape=jax.ShapeDtypeStruct((n, ncols), F32),
        compiler_params=_cparams("parallel", "arbitrary"),
        name="in_proj",
    )(a, w_stack)


def _outproj_kernel(ap_ref, as_ref, w_ref, xp_ref, xs_ref, nw_ref, rh_ref, rl_ref, x1_ref, lg_ref, acc_ref,
                    *, n_prompt_tiles, nk, tk):
    i = pl.program_id(0)
    k = pl.program_id(1)
    w = w_ref[pl.ds(pl.multiple_of(k * tk, tk), tk), :]

    @pl.when(i < n_prompt_tiles)
    def _():
        @pl.when(k == 0)
        def _():
            acc_ref[...] = xp_ref[...]

        acc_ref[...] += _dot(ap_ref[...], w)

    @pl.when(i >= n_prompt_tiles)
    def _():
        @pl.when(k == 0)
        def _():
            acc_ref[...] = xs_ref[...]

        acc_ref[...] += _dot(as_ref[...].astype(BF16), w)

    @pl.when(k == nk - 1)
    def _():
        x1 = acc_ref[...]
        x1_ref[...] = x1
        ms = jnp.mean(x1 * x1, axis=-1, keepdims=True)
        xn = x1 * lax.rsqrt(ms + RMS_EPS) * nw_ref[...]
        hi = xn.astype(BF16)
        lo = (xn - hi.astype(F32)).astype(BF16)
        rh = rh_ref[...]
        lg_ref[...] = _dot(hi, rh) + (_dot(hi, rl_ref[...]) + _dot(lo, rh))


def _outproj(act_p, act_s, w_bf16, x_p, x_s, norm_w, r_hi, r_lo, tk):
    n_p, kdim = act_p.shape
    d = x_p.shape[1]
    tm = act_s.shape[0]
    n = n_p + tm
    npt = n_p // tm
    nk = kdim // tk
    kern = functools.partial(_outproj_kernel, n_prompt_tiles=npt, nk=nk, tk=tk)
    return pl.pallas_call(
        kern,
        grid=(n // tm, nk),
        in_specs=[
            pl.BlockSpec((tm, tk), lambda i, k: (jnp.minimum(i, npt - 1), jnp.where(i < npt, k, nk - 1))),
            pl.BlockSpec((tm, tk), lambda i, k: (0, jnp.where(i >= npt, k, 0))),
            pl.BlockSpec((kdim, d), lambda i, k: (0, 0), pipeline_mode=pl.Buffered(1)),
            pl.BlockSpec((tm, d), lambda i, k: (jnp.minimum(i, npt - 1), 0)),
            pl.BlockSpec((tm, d), lambda i, k: (0, 0), pipeline_mode=pl.Buffered(1)),
            pl.BlockSpec((1, d), lambda i, k: (0, 0)),
            pl.BlockSpec((d, LANES), lambda i, k: (0, 0)),
            pl.BlockSpec((d, LANES), lambda i, k: (0, 0)),
        ],
        out_specs=[
            pl.BlockSpec((tm, d), lambda i, k: (i, 0)),
            pl.BlockSpec((tm, LANES), lambda i, k: (i, 0)),
        ],
        out_shape=[
            jax.ShapeDtypeStruct((n, d), F32),
            jax.ShapeDtypeStruct((n, LANES), F32),
        ],
        scratch_shapes=[pltpu.VMEM((tm, d), F32)],
        compiler_params=_cparams("arbitrary", "arbitrary"),
        name="out_proj",
    )(act_p, act_s, w_bf16, x_p, x_s, norm_w.reshape(1, d), r_hi, r_lo)


def _causal_conv(carry_ref, x, w_ref, bias, carry0_ref, width, rows, first):
    @pl.when(first)
    def _():
        carry_ref[...] = carry0_ref[...]

    n_tiles = rows // SUBLANES
    tiles = [carry_ref[...]] + [x[i * SUBLANES:(i + 1) * SUBLANES, :] for i in range(n_tiles)]
    sub = lax.broadcasted_iota(jnp.int32, tiles[0].shape, 0)
    y = None
    for k in range(width):
        back = width - 1 - k
        if back == 0:
            tap = x
        else:
            rolled = [pltpu.roll(t, back, 0) for t in tiles]
            tap = jnp.concatenate(
                [jnp.where(sub < back, rolled[i], rolled[i + 1]) for i in range(n_tiles)], axis=0)
        term = tap * w_ref[k:k + 1, :]
        y = term if y is None else y + term
    if bias is not None:
        y = y + bias
    carry_ref[...] = tiles[-1]
    return y


def _pad_rows(a, rows):
    if a.shape[0] == rows:
        return a
    return jnp.concatenate([a, jnp.zeros((rows - a.shape[0], a.shape[1]), a.dtype)], axis=0)


def _loader(packed, t_valid):
    if not packed:
        return lambda ref: ref[...]
    second = lax.rem(pl.program_id(0), 2) == 1

    def load(ref):
        x8 = ref[...]
        x8 = jnp.where(second, pltpu.roll(x8, SUBLANES - t_valid, 0), x8)
        sub = lax.broadcasted_iota(jnp.int32, x8.shape, 0)
        x8 = jnp.where(sub < t_valid, x8, 0.0)
        return jnp.concatenate([x8, jnp.zeros((SAMPLE_PAD_T - SUBLANES, x8.shape[1]), x8.dtype)], axis=0)

    return load


def _row_blocks(nseq, rows, n_chunks, t_valid, packed_row0):
    if packed_row0 is None:
        last = nseq * n_chunks - 1
        rmap = lambda b, c: (jnp.where(b < nseq, b * n_chunks + c, last), 0)
        return rows, rmap, rmap, nseq * n_chunks * rows, BF16
    assert n_chunks == 1 and 2 * t_valid == SUBLANES and nseq % 2 == 0 and packed_row0 % SUBLANES == 0
    blk0 = packed_row0 // SUBLANES
    imap = lambda b, c: (blk0 + jnp.minimum(b, nseq - 1) // 2, 0)
    omap = lambda b, c: (jnp.minimum(b, nseq - 1) // 2, 0)
    return SUBLANES, imap, omap, nseq * t_valid, F32


def _store_act(act_ref, act, packed, t_valid):
    if not packed:
        act_ref[...] = act.astype(act_ref.dtype)
        return
    second = lax.rem(pl.program_id(0), 2) == 1
    a8 = act[0:SUBLANES, :].astype(act_ref.dtype)

    @pl.when(jnp.logical_not(second))
    def _():
        act_ref[...] = a8

    @pl.when(second)
    def _():
        sub = lax.broadcasted_iota(jnp.int32, a8.shape, 0)
        act_ref[...] = jnp.where(sub < t_valid, act_ref[...], pltpu.roll(a8, t_valid, 0))


def _ssd_kernel(*refs, nseq, n_fill, aliased, **static):
    if aliased:
        refs = refs[:11] + refs[12:]
    if n_fill == 0:
        _ssd_body(*refs, **static)
        return
    ht_ref = refs[12]

    @pl.when(pl.program_id(0) < nseq)
    def _():
        _ssd_body(*refs, **static)

    @pl.when(pl.program_id(0) >= nseq)
    def _():
        ht_ref[...] = jnp.zeros_like(ht_ref)


def _ssd_body(z_ref, xbc_ref, dt_ref, carry0_ref, h0_ref, cw_ref, cb_ref, dtb_ref, alog_ref, dexp_ref, nw_ref,
              act_ref, ht_ref, xp_ref, h_ref, *, rows, n_chunks, t_valid, packed):
    c = pl.program_id(1)
    lp = SSM_CHUNK
    load = _loader(packed, t_valid)

    @pl.when(c == 0)
    def _():
        h_ref[...] = h0_ref[...]

    conv = _causal_conv(xp_ref, load(xbc_ref), cw_ref, cb_ref[...], carry0_ref, SSM_CONV, rows, c == 0)
    xbc = _silu(conv)
    xs = xbc[:, :SSM_INNER]
    bm = xbc[:, SSM_INNER:SSM_INNER + SSM_BC]
    cm = xbc[:, SSM_INNER + SSM_BC:]

    row_q = lax.broadcasted_iota(jnp.int32, (rows, LANES), 0)
    dt = _softplus(load(dt_ref) + dtb_ref[...])
    if t_valid < rows:
        dt = jnp.where(row_q < t_valid, dt, 0.0)
    a_neg = -jnp.exp(alog_ref[...])
    dt_p = _pad_rows(dt, lp)
    ad = dt_p * a_neg
    r_i = lax.broadcasted_iota(jnp.int32, (lp, lp), 0)
    c_i = lax.broadcasted_iota(jnp.int32, (lp, lp), 1)
    tri = jnp.where(c_i <= r_i, 1.0, 0.0).astype(BF16)
    a_hi, a_mid, a_lo = _split3(ad)
    cs = _dot(tri, a_hi) + (_dot(tri, a_mid) + _dot(tri, a_lo))
    cs2 = cs * LOG2_E
    cs_t = cs2.T
    dt_t = dt_p.T
    cs_last = cs[lp - 1:lp, :]
    to_end = jnp.exp(cs_last - cs)
    w_t = (dt_p * to_end).T
    e_last_t = jnp.broadcast_to(jnp.exp(cs_last), (lp, LANES)).T
    cs_q = cs2[:rows, :]
    lane_q = lax.broadcasted_iota(jnp.int32, (rows, LANES), 1)
    causal = lax.broadcasted_iota(jnp.int32, (rows, lp), 1) <= lax.broadcasted_iota(jnp.int32, (rows, lp), 0)
    lane_lo = lane_q < SSM_HEAD_DIM

    y_groups = []
    for g in range(SSM_GROUPS):
        cg = cm[:, g * SSM_STATE:(g + 1) * SSM_STATE].astype(BF16)
        bg = _pad_rows(bm[:, g * SSM_STATE:(g + 1) * SSM_STATE], lp).astype(BF16)
        gw = SSM_HPG * SSM_HEAD_DIM
        hg = h_ref[g * gw:(g + 1) * gw, :]
        cb = _dot_nt(cg, bg)
        yoff = _dot_nt(cg, hg.astype(BF16))
        xs_g = _pad_rows(xs[:, g * gw:(g + 1) * gw], lp)
        y_pairs = []
        for j in range(SSM_HPG // 2):
            ha = g * SSM_HPG + 2 * j
            hb = ha + 1
            col_a = jnp.sum(jnp.where(lane_q == ha, cs_q, 0.0), axis=1, keepdims=True)
            col_b = jnp.sum(jnp.where(lane_q == hb, cs_q, 0.0), axis=1, keepdims=True)
            m_a = cb * jnp.exp2(jnp.where(causal, col_a - cs_t[ha:ha + 1, :], -jnp.inf)) * dt_t[ha:ha + 1, :]
            m_b = cb * jnp.exp2(jnp.where(causal, col_b - cs_t[hb:hb + 1, :], -jnp.inf)) * dt_t[hb:hb + 1, :]
            xs_pair = xs_g[:, j * LANES:(j + 1) * LANES].astype(BF16)
            ydiag = jnp.where(lane_lo, _dot(m_a.astype(BF16), xs_pair), _dot(m_b.astype(BF16), xs_pair))
            scale = jnp.where(lane_lo, jnp.exp2(col_a), jnp.exp2(col_b))
            y_pairs.append(ydiag + yoff[:, j * LANES:(j + 1) * LANES] * scale)
        y_groups.append(jnp.concatenate(y_pairs, axis=1))
        xs_t = xs_g.T
        w_rows = jnp.concatenate(
            [jnp.broadcast_to(w_t[g * SSM_HPG + j:g * SSM_HPG + j + 1, :], (SSM_HEAD_DIM, lp)) for j in range(SSM_HPG)],
            axis=0)
        d_rows = jnp.concatenate(
            [jnp.broadcast_to(e_last_t[g * SSM_HPG + j:g * SSM_HPG + j + 1, :], (SSM_HEAD_DIM, LANES))
             for j in range(SSM_HPG)], axis=0)
        st = _dot((xs_t * w_rows).astype(BF16), bg)
        h_ref[g * gw:(g + 1) * gw, :] = hg * d_rows + st

    y = jnp.concatenate(y_groups, axis=1)
    y = y + xs * dexp_ref[...]
    y = y * _silu(load(z_ref))
    gsz = SSM_INNER // SSM_GROUPS
    outs = []
    for g in range(SSM_GROUPS):
        yg = y[:, g * gsz:(g + 1) * gsz]
        ms = jnp.mean(yg * yg, axis=-1, keepdims=True)
        outs.append(yg * lax.rsqrt(ms + RMS_EPS))
    _store_act(act_ref, jnp.concatenate(outs, axis=1) * nw_ref[...], packed, t_valid)

    @pl.when(c == n_chunks - 1)
    def _():
        ht_ref[...] = h_ref[...]


def _ssd_core(z, xbc, dt_raw, carry0, h0, conv_w, conv_b, dt_bias, a_log, d_exp, norm_w, ht_buf, *, nseq, rows,
              n_chunks, t_valid, seq0, n_slots, packed_row0=None):
    aliased = ht_buf is not None
    packed = packed_row0 is not None
    n_fill = 0 if aliased else n_slots - seq0 - nseq
    kern = functools.partial(_ssd_kernel, nseq=nseq, n_fill=n_fill, aliased=aliased, rows=rows, n_chunks=n_chunks,
                             t_valid=t_valid, packed=packed)
    blk_rows, rmap, omap, act_rows, act_dtype = _row_blocks(nseq, rows, n_chunks, t_valid, packed_row0)
    smap = lambda b, c: (jnp.minimum(b, nseq - 1), 0, 0)
    const = lambda b, c: (0, 0)
    in_specs = [
        pl.BlockSpec((blk_rows, SSM_INNER), rmap),
        pl.BlockSpec((blk_rows, SSM_CONV_DIM), rmap),
        pl.BlockSpec((blk_rows, LANES), rmap),
        pl.BlockSpec((None, SUBLANES, SSM_CONV_DIM), smap),
        pl.BlockSpec((None, SSM_INNER, SSM_STATE), lambda b, c: (seq0 + jnp.minimum(b, nseq - 1), 0, 0)),
        pl.BlockSpec((SSM_CONV, SSM_CONV_DIM), const),
        pl.BlockSpec((1, SSM_CONV_DIM), const),
        pl.BlockSpec((1, LANES), const),
        pl.BlockSpec((1, LANES), const),
        pl.BlockSpec((1, SSM_INNER), const),
        pl.BlockSpec((1, SSM_INNER), const),
    ]
    args = [z, xbc, dt_raw, carry0, h0, conv_w, conv_b, dt_bias, a_log, d_exp, norm_w]
    if aliased:
        in_specs.append(pl.BlockSpec(memory_space=pl.ANY))
        args.append(ht_buf)
    return pl.pallas_call(
        kern,
        grid=(nseq + n_fill, n_chunks),
        in_specs=in_specs,
        out_specs=[
            pl.BlockSpec((blk_rows, SSM_INNER), omap),
            pl.BlockSpec((None, SSM_INNER, SSM_STATE), lambda b, c: (seq0 + b, 0, 0)),
        ],
        out_shape=[
            jax.ShapeDtypeStruct((act_rows, SSM_INNER), act_dtype),
            jax.ShapeDtypeStruct((n_slots, SSM_INNER, SSM_STATE), F32),
        ],
        scratch_shapes=[
            pltpu.VMEM((SUBLANES, SSM_CONV_DIM), F32),
            pltpu.VMEM((SSM_INNER, SSM_STATE), F32),
        ],
        input_output_aliases={11: 1} if aliased else {},
        compiler_params=_cparams("arbitrary", "arbitrary"),
        name="ssd_core",
    )(*args)


def _lru_kernel(gin_ref, xr_ref, carry0_ref, h0_ref, cw_ref, cb_ref, wa_ref, ba_ref, wi_ref, bi_ref, lam_ref,
                act_ref, ht_ref, xp_ref, hc_ref, *, rows, n_chunks, t_valid, packed):
    c = pl.program_id(1)
    load = _loader(packed, t_valid)

    @pl.when(c == 0)
    def _():
        hc_ref[...] = h0_ref[...]

    xc = _causal_conv(xp_ref, load(xr_ref), cw_ref, cb_ref[...], carry0_ref, LRU_CONV, rows, c == 0)
    ra = []
    ia = []
    for h in range(LRU_HEADS):
        xh = xc[:, h * LRU_BLOCK:(h + 1) * LRU_BLOCK].astype(BF16)
        ra.append(_dot(xh, wa_ref[h].astype(BF16)))
        ia.append(_dot(xh, wi_ref[h].astype(BF16)))
    r = jax.nn.sigmoid(jnp.concatenate(ra, axis=1) + ba_ref[...])
    ig = jax.nn.sigmoid(jnp.concatenate(ia, axis=1) + bi_ref[...])
    log_a = -LRU_C * r * _softplus(-lam_ref[...])
    a = jnp.exp(log_a)
    th = jnp.tanh(log_a)
    mult = jnp.sqrt(-2.0 * th / (1.0 - th))
    u = mult * (ig * xc)
    if t_valid < rows:
        row = lax.broadcasted_iota(jnp.int32, (rows, LRU_WIDTH), 0)
        a = jnp.where(row < t_valid, a, 1.0)
        u = jnp.where(row < t_valid, u, 0.0)

    sub = lax.broadcasted_iota(jnp.int32, (SUBLANES, LRU_WIDTH), 0)
    h_prev = hc_ref[...]
    h_blocks = []
    for blk in range(rows // SUBLANES):
        ab = a[blk * SUBLANES:(blk + 1) * SUBLANES, :]
        ub = u[blk * SUBLANES:(blk + 1) * SUBLANES, :]
        for d in (1, 2, 4):
            a_sh = jnp.where(sub >= d, pltpu.roll(ab, d, 0), 1.0)
            u_sh = jnp.where(sub >= d, pltpu.roll(ub, d, 0), 0.0)
            ub = ub + ab * u_sh
            ab = ab * a_sh
        hb = ub + ab * h_prev
        h_blocks.append(hb)
        h_prev = hb[SUBLANES - 1:SUBLANES, :]
    hc_ref[...] = h_prev
    hseq = jnp.concatenate(h_blocks, axis=0)
    gx = load(gin_ref)
    gate = 0.5 * gx * (1.0 + jnp.tanh(math.sqrt(2.0 / math.pi) * (gx + 0.044715 * (gx * gx * gx))))
    _store_act(act_ref, hseq * gate, packed, t_valid)

    @pl.when(c == n_chunks - 1)
    def _():
        ht_ref[...] = hc_ref[...]


def _lru_core(proj, carry0, h0, conv_w, conv_b, wa, ba, wi, bi, lam, *, nseq, rows, n_chunks, t_valid,
              packed_row0=None):
    kern = functools.partial(_lru_kernel, rows=rows, n_chunks=n_chunks, t_valid=t_valid,
                             packed=packed_row0 is not None)
    blk_rows, rmap, omap, act_rows, act_dtype = _row_blocks(nseq, rows, n_chunks, t_valid, packed_row0)
    w = LRU_WIDTH
    const = lambda b, c: (0, 0)
    const3 = lambda b, c: (0, 0, 0)
    return pl.pallas_call(
        kern,
        grid=(nseq, n_chunks),
        in_specs=[
            pl.BlockSpec((blk_rows, w), rmap),
            pl.BlockSpec((blk_rows, w), lambda b, c: (rmap(b, c)[0], 1)),
            pl.BlockSpec((None, SUBLANES, w), lambda b, c: (b, 0, 0)),
            pl.BlockSpec((None, 1, w), lambda b, c: (b, 0, 0)),
            pl.BlockSpec((LRU_CONV, w), const),
            pl.BlockSpec((1, w), const),
            pl.BlockSpec((LRU_HEADS, LRU_BLOCK, LRU_BLOCK), const3),
            pl.BlockSpec((1, w), const),
            pl.BlockSpec((LRU_HEADS, LRU_BLOCK, LRU_BLOCK), const3),
            pl.BlockSpec((1, w), const),
            pl.BlockSpec((1, w), const),
        ],
        out_specs=[
            pl.BlockSpec((blk_rows, w), omap),
            pl.BlockSpec((None, 1, w), lambda b, c: (b, 0, 0)),
        ],
        out_shape=[
            jax.ShapeDtypeStruct((act_rows, w), act_dtype),
            jax.ShapeDtypeStruct((nseq, 1, w), F32),
        ],
        scratch_shapes=[pltpu.VMEM((SUBLANES, w), F32), pltpu.VMEM((1, w), F32)],
        compiler_params=_cparams("arbitrary", "arbitrary"),
        name="lru_core",
    )(proj, proj, carry0, h0, conv_w, conv_b, wa, ba, wi, bi, lam)


def _sc_kernel(bg_ref, cg_ref, hh_ref, carry0_ref, cw_ref, act_ref, vt_ref, xp_ref, *, rows, n_chunks, t_valid,
               packed):
    c = pl.program_id(1)
    load = _loader(packed, t_valid)
    v = load(cg_ref) * load(hh_ref)
    u = _causal_conv(xp_ref, v, cw_ref, None, carry0_ref, SC_WIDTH, rows, c == 0)
    _store_act(act_ref, load(bg_ref) * u, packed, t_valid)
    lo = max(t_valid - SUBLANES, 0)

    @pl.when(c == n_chunks - 1)
    def _():
        vt_ref[...] = v[lo:lo + SUBLANES, :]


def _sc_core(proj, carry0, conv_w, *, nseq, rows, n_chunks, t_valid, packed_row0=None):
    kern = functools.partial(_sc_kernel, rows=rows, n_chunks=n_chunks, t_valid=t_valid,
                             packed=packed_row0 is not None)
    blk_rows, rmap, omap, act_rows, act_dtype = _row_blocks(nseq, rows, n_chunks, t_valid, packed_row0)
    d = D_MODEL
    return pl.pallas_call(
        kern,
        grid=(nseq, n_chunks),
        in_specs=[
            pl.BlockSpec((blk_rows, d), rmap),
            pl.BlockSpec((blk_rows, d), lambda b, c: (rmap(b, c)[0], 1)),
            pl.BlockSpec((blk_rows, d), lambda b, c: (rmap(b, c)[0], 2)),
            pl.BlockSpec((None, SUBLANES, d), lambda b, c: (b, 0, 0)),
            pl.BlockSpec((SC_WIDTH, d), lambda b, c: (0, 0)),
        ],
        out_specs=[
            pl.BlockSpec((blk_rows, d), omap),
            pl.BlockSpec((None, SUBLANES, d), lambda b, c: (b, 0, 0)),
        ],
        out_shape=[
            jax.ShapeDtypeStruct((act_rows, d), act_dtype),
            jax.ShapeDtypeStruct((nseq, SUBLANES, d), F32),
        ],
        scratch_shapes=[pltpu.VMEM((SUBLANES, d), F32)],
        compiler_params=_cparams("arbitrary", "arbitrary"),
        name="sconv_core",
    )(proj, proj, proj, carry0, conv_w)


def _route_kernel(lg_ref, info_ref, w12_ref, cnt_ref, carry_ref, *, tm, n_tiles):
    i = pl.program_id(0)

    @pl.when(i == 0)
    def _():
        carry_ref[...] = jnp.zeros_like(carry_ref)

    lg = lg_ref[...]
    lane = lax.broadcasted_iota(jnp.int32, (tm, LANES), 1).astype(F32)
    neg = -jnp.inf
    big = float(LANES)
    is_g = lane < MOE_GROUPS
    gl = jnp.where(is_g, lg, neg)
    gmax = jnp.max(gl, axis=1, keepdims=True)
    gidx = jnp.min(jnp.where(gl == gmax, lane, big), axis=1, keepdims=True)
    gsum = jnp.sum(jnp.where(is_g, jnp.exp(lg - gmax), 0.0), axis=1, keepdims=True)
    gw = 1.0 / gsum
    rel = lane - (MOE_GROUPS + MOE_PER_GROUP * gidx)
    in_grp = (rel >= 0.0) & (rel < MOE_PER_GROUP)
    el = jnp.where(in_grp, lg, neg)
    v1 = jnp.max(el, axis=1, keepdims=True)
    i1 = jnp.min(jnp.where(in_grp & (el == v1), lane, big), axis=1, keepdims=True)
    in_grp2 = in_grp & (lane != i1)
    el2 = jnp.where(in_grp2, lg, neg)
    v2 = jnp.max(el2, axis=1, keepdims=True)
    i2 = jnp.min(jnp.where(in_grp2 & (el2 == v2), lane, big), axis=1, keepdims=True)
    t = jnp.exp(v2 - v1)
    den = 1.0 + t
    w1 = (1.0 / den) * gw
    w2 = (t / den) * gw
    e1 = i1 - MOE_GROUPS
    e2 = i2 - MOE_GROUPS
    onehot = jnp.where(lane == e1, 1.0, jnp.where(lane == e2, 1.0, 0.0))
    r_i = lax.broadcasted_iota(jnp.int32, (tm, tm), 0)
    c_i = lax.broadcasted_iota(jnp.int32, (tm, tm), 1)
    tri = jnp.where(c_i < r_i, 1.0, 0.0).astype(BF16)
    before = _dot(tri, onehot.astype(BF16)) + carry_ref[0:1, :]
    r1 = jnp.sum(jnp.where(lane == e1, before, 0.0), axis=1, keepdims=True)
    r2 = jnp.sum(jnp.where(lane == e2, before, 0.0), axis=1, keepdims=True)
    carry_ref[0:1, :] = carry_ref[0:1, :] + jnp.sum(onehot, axis=0, keepdims=True)
    info_ref[...] = jnp.where(lane == 0, e1, jnp.where(lane == 1, e2, jnp.where(lane == 2, r1, jnp.where(
        lane == 3, r2, 0.0))))
    w12_ref[...] = jnp.concatenate(
        [jnp.broadcast_to(w1, (tm, LANES)), jnp.broadcast_to(w2, (tm, LANES))], axis=1)

    @pl.when(i == n_tiles - 1)
    def _():
        cnt_ref[...] = carry_ref[...]


def _route(logits, tm):
    n = logits.shape[0]
    n_tiles = n // tm
    kern = functools.partial(_route_kernel, tm=tm, n_tiles=n_tiles)
    return pl.pallas_call(
        kern,
        grid=(n_tiles,),
        in_specs=[pl.BlockSpec((tm, LANES), lambda i: (i, 0))],
        out_specs=[
            pl.BlockSpec((tm, LANES), lambda i: (i, 0)),
            pl.BlockSpec((tm, 2 * LANES), lambda i: (i, 0)),
            pl.BlockSpec((SUBLANES, LANES), lambda i: (0, 0)),
        ],
        out_shape=[
            jax.ShapeDtypeStruct((n, LANES), F32),
            jax.ShapeDtypeStruct((n, 2 * LANES), F32),
            jax.ShapeDtypeStruct((SUBLANES, LANES), F32),
        ],
        scratch_shapes=[pltpu.VMEM((SUBLANES, LANES), F32)],
        compiler_params=_cparams("arbitrary"),
        name="route",
    )(logits)


def _dispatch_kernel(d0_ref, d1_ref, x_ref, nw_ref, xg_ref, xn_ref, sem, *, tm):
    base = pl.program_id(0) * tm
    x = x_ref[...]
    ms = jnp.mean(x * x, axis=-1, keepdims=True)
    xn_ref[...] = x * lax.rsqrt(ms + RMS_EPS) * nw_ref[...]

    def row_copy(r, d):
        return pltpu.make_async_copy(xn_ref.at[pl.ds(r, 1), :], xg_ref.at[pl.ds(d, 1), :], sem)

    def issue(g, carry):
        r0 = pl.multiple_of(g * SUBLANES, SUBLANES)
        for j in range(SUBLANES):
            row_copy(r0 + j, d0_ref[base + r0 + j]).start(priority=0)
            row_copy(r0 + j, d1_ref[base + r0 + j]).start(priority=1)
        return carry

    lax.fori_loop(0, tm // SUBLANES, issue, 0)

    def drain(g, carry):
        for _ in range(2 * SUBLANES):
            row_copy(0, 0).wait()
        return carry

    lax.fori_loop(0, tm // SUBLANES, drain, 0)


def _dispatch(x1, norm_w, dest0, dest1, tm):
    n, d = x1.shape
    grid_spec = pltpu.PrefetchScalarGridSpec(
        num_scalar_prefetch=2,
        grid=(n // tm,),
        in_specs=[pl.BlockSpec((tm, d), lambda i, *_: (i, 0)), pl.BlockSpec((1, d), lambda i, *_: (0, 0))],
        out_specs=pl.BlockSpec(memory_space=pl.ANY),
        scratch_shapes=[pltpu.VMEM((tm, d), F32), pltpu.SemaphoreType.DMA(())],
    )
    return pl.pallas_call(
        functools.partial(_dispatch_kernel, tm=tm),
        grid_spec=grid_spec,
        out_shape=jax.ShapeDtypeStruct((2 * n, d), F32),
        compiler_params=_cparams("arbitrary"),
        name="moe_dispatch",
    )(dest0, dest1, x1, norm_w.reshape(1, d))


VISIT_VALID = 1
VISIT_FIRST_OF_EXPERT = 2
VISIT_SLOT = 4


def _ffn_kernel(vt_ref, ve_ref, vf_ref, vn_ref, off_ref, x_ref, wg_hbm, wu_hbm, wd_hbm, o_ref,
                sg_ref, su_ref, sd_ref, cg_ref, cu_ref, cd_ref, sem, *, tm, layer):
    v = pl.program_id(0)
    t = vt_ref[v]
    e = ve_ref[v]
    flags = vf_ref[v]
    first_of_tile = jnp.logical_or(v == 0, vt_ref[jnp.maximum(v - 1, 0)] != t)

    @pl.when(first_of_tile)
    def _():
        o_ref[...] = jnp.zeros_like(o_ref)

    def copies(expert, slot):
        return (pltpu.make_async_copy(wg_hbm.at[layer, expert], sg_ref.at[slot], sem.at[slot, 0]),
                pltpu.make_async_copy(wu_hbm.at[layer, expert], su_ref.at[slot], sem.at[slot, 1]),
                pltpu.make_async_copy(wd_hbm.at[layer, expert], sd_ref.at[slot], sem.at[slot, 2]))

    @pl.when((flags & VISIT_FIRST_OF_EXPERT) != 0)
    def _():
        slot = (flags & VISIT_SLOT) // VISIT_SLOT
        nxt = vn_ref[v]

        @pl.when(v == 0)
        def _():
            for c in copies(e, slot):
                c.start()

        @pl.when(nxt >= 0)
        def _():
            for c in copies(nxt, 1 - slot):
                c.start()

        for c in copies(e, slot):
            c.wait()
        cg_ref[...] = sg_ref[slot].astype(BF16)
        cu_ref[...] = su_ref[slot].astype(BF16)
        cd_ref[...] = sd_ref[slot].astype(BF16)

    @pl.when((flags & VISIT_VALID) != 0)
    def _():
        row = t * tm + lax.broadcasted_iota(jnp.int32, (tm, MOE_FF), 0)
        mine = (row >= off_ref[e]) & (row < off_ref[e + 1])
        x = x_ref[...].astype(BF16)
        hg = _dot(x, cg_ref[...])
        hu = _dot(x, cu_ref[...])
        hid = jnp.where(mine, _silu(hg) * hu, 0.0)
        o_ref[...] += _dot(hid.astype(BF16), cd_ref[...])


def _ffn(xg, w_gate, w_up, w_down, layer, plan, tm):
    rows, d = xg.shape
    n_visits = plan[0].shape[0]
    kern = functools.partial(_ffn_kernel, tm=tm, layer=layer)
    grid_spec = pltpu.PrefetchScalarGridSpec(
        num_scalar_prefetch=5,
        grid=(n_visits,),
        in_specs=[
            pl.BlockSpec((tm, d), lambda v, vt, *_: (vt[v], 0)),
            pl.BlockSpec(memory_space=pl.ANY),
            pl.BlockSpec(memory_space=pl.ANY),
            pl.BlockSpec(memory_space=pl.ANY),
        ],
        out_specs=pl.BlockSpec((tm, d), lambda v, vt, *_: (vt[v], 0)),
        scratch_shapes=[
            pltpu.VMEM((2, d, MOE_FF), F32), pltpu.VMEM((2, d, MOE_FF), F32), pltpu.VMEM((2, MOE_FF, d), F32),
            pltpu.VMEM((d, MOE_FF), BF16), pltpu.VMEM((d, MOE_FF), BF16), pltpu.VMEM((MOE_FF, d), BF16),
            pltpu.SemaphoreType.DMA((2, 3)),
        ],
    )
    return pl.pallas_call(
        kern,
        grid_spec=grid_spec,
        out_shape=jax.ShapeDtypeStruct((rows, d), F32),
        compiler_params=_cparams("arbitrary"),
        name="moe_ffn",
    )(*plan, xg, w_gate, w_up, w_down)


def _visit_plan(counts, n_tiles, tm):
    off = jnp.concatenate([jnp.zeros((1,), jnp.int32), jnp.cumsum(counts)])
    n_visits = n_tiles + N_EXPERTS - 1
    first_tile = off[:-1] // tm
    last_tile = jnp.where(counts > 0, (off[1:] - 1) // tm, first_tile)
    n_e = jnp.where(counts > 0, last_tile - first_tile + 1, 0)
    v_off = jnp.concatenate([jnp.zeros((1,), jnp.int32), jnp.cumsum(n_e)])
    total = v_off[-1]
    vid = jnp.arange(n_visits, dtype=jnp.int32)
    exp_of = jnp.sum((vid[:, None] >= v_off[None, 1:]).astype(jnp.int32), axis=1)
    exp_of = jnp.minimum(exp_of, N_EXPERTS - 1)
    tile_of = first_tile[exp_of] + (vid - v_off[exp_of])
    valid = vid < total
    first_of_expert = valid & (vid == v_off[exp_of])
    used = counts > 0
    order = jnp.cumsum(used.astype(jnp.int32)) - 1
    ids = jnp.arange(N_EXPERTS, dtype=jnp.int32)
    later = jnp.where(used[None, :] & (ids[None, :] > ids[:, None]), ids[None, :], N_EXPERTS)
    next_used = jnp.min(later, axis=1)
    next_used = jnp.where(next_used < N_EXPERTS, next_used, -1)
    last_e = jnp.minimum(jnp.sum((total - 1 >= v_off[1:]).astype(jnp.int32)), N_EXPERTS - 1)
    exp_of = jnp.where(valid, exp_of, last_e)
    tile_of = jnp.where(valid, tile_of, n_tiles - 1)
    flags = valid * VISIT_VALID + first_of_expert * VISIT_FIRST_OF_EXPERT + (order[exp_of] % 2) * VISIT_SLOT
    i32 = lambda a: a.astype(jnp.int32)
    return i32(tile_of), i32(exp_of), i32(flags), i32(next_used[exp_of]), i32(off)


def _combine_kernel(d0_ref, d1_ref, x1_ref, w12_ref, nw_ref, yg_ref, *refs, tm, n_tiles, n_prompt_tiles, final):
    outs, (buf, sem) = refs[:-2], refs[-2:]
    i = pl.program_id(0)

    def row_copy(slot, k, r, d):
        return pltpu.make_async_copy(yg_ref.at[pl.ds(d, 1), :], buf.at[slot, k, pl.ds(r, 1), :], sem.at[slot])

    def issue(tile, slot):
        base = tile * tm

        def body(g, carry):
            r0 = pl.multiple_of(g * SUBLANES, SUBLANES)
            for j in range(SUBLANES):
                row_copy(slot, 0, r0 + j, d0_ref[base + r0 + j]).start(priority=0)
                row_copy(slot, 1, r0 + j, d1_ref[base + r0 + j]).start(priority=1)
            return carry

        lax.fori_loop(0, tm // SUBLANES, body, 0)

    @pl.when(i == 0)
    def _():
        issue(0, 0)

    @pl.when(i + 1 < n_tiles)
    def _():
        issue(i + 1, lax.rem(i + 1, 2))

    slot = lax.rem(i, 2)

    def drain(g, carry):
        for _ in range(SUBLANES):
            row_copy(slot, 0, 0, 0).wait()
            row_copy(slot, 1, 0, 0).wait()
        return carry

    lax.fori_loop(0, tm // SUBLANES, drain, 0)

    w12 = w12_ref[...]
    reps = x1_ref.shape[1] // LANES
    w1 = jnp.concatenate([w12[:, :LANES]] * reps, axis=1)
    w2 = jnp.concatenate([w12[:, LANES:]] * reps, axis=1)
    x2 = x1_ref[...] + (w1 * buf[slot, 0] + w2 * buf[slot, 1])
    ms = jnp.mean(x2 * x2, axis=-1, keepdims=True)
    xn = x2 * lax.rsqrt(ms + RMS_EPS) * nw_ref[...]
    if final:
        split, split_val = outs, xn
    else:
        split, split_val = outs[:2], x2
        outs[2][...] = xn.astype(outs[2].dtype)

    @pl.when(i < n_prompt_tiles)
    def _():
        split[0][...] = split_val

    @pl.when(i >= n_prompt_tiles)
    def _():
        split[1][...] = split_val


def _combine(x1, w12, yg, dest0, dest1, norm_w, n_p, tm, final):
    n, d = x1.shape
    n_tiles = n // tm
    npt = n_p // tm
    spec_p = pl.BlockSpec((tm, d), lambda i, *_: (jnp.minimum(i, npt - 1), 0))
    spec_s = pl.BlockSpec((tm, d), lambda i, *_: (jnp.maximum(i - npt, 0), 0))
    out_specs = [spec_p, spec_s]
    out_shape = [jax.ShapeDtypeStruct((n_p, d), F32), jax.ShapeDtypeStruct((n - n_p, d), F32)]
    if not final:
        out_specs.append(pl.BlockSpec((tm, d), lambda i, *_: (i, 0)))
        out_shape.append(jax.ShapeDtypeStruct((n, d), BF16))
    grid_spec = pltpu.PrefetchScalarGridSpec(
        num_scalar_prefetch=2,
        grid=(n_tiles,),
        in_specs=[
            pl.BlockSpec((tm, d), lambda i, *_: (i, 0)),
            pl.BlockSpec((tm, 2 * LANES), lambda i, *_: (i, 0)),
            pl.BlockSpec((1, d), lambda i, *_: (0, 0)),
            pl.BlockSpec(memory_space=pl.ANY),
        ],
        out_specs=out_specs,
        scratch_shapes=[pltpu.VMEM((2, 2, tm, d), F32), pltpu.SemaphoreType.DMA((2,))],
    )
    return pl.pallas_call(
        functools.partial(_combine_kernel, tm=tm, n_tiles=n_tiles, n_prompt_tiles=npt, final=final),
        grid_spec=grid_spec,
        out_shape=out_shape,
        compiler_params=_cparams("arbitrary"),
        name="moe_combine",
    )(dest0, dest1, x1, w12, norm_w.reshape(1, d), yg)


def _pad_lanes(v, width=LANES):
    v = v.reshape(1, -1)
    return jnp.pad(v, ((0, 0), (0, width - v.shape[1])))


def _carry_rows(state, width):
    return jnp.pad(state, ((0, 0), (SUBLANES - (width - 1), 0), (0, 0)))


def _moe(x1, ffn_norm_w, logits, layer, w_gate, w_up, w_down, next_norm_w, n_p, final):
    n = x1.shape[0]
    info, w12, cnt = _route(logits, ROUTE_TM)
    counts = cnt[0, :N_EXPERTS].astype(jnp.int32)
    er = info[:, 0:4].T.astype(jnp.int32)
    plan = _visit_plan(counts, (2 * n) // FFN_TM, FFN_TM)
    off = plan[-1]
    dest0 = off[er[0]] + er[2]
    dest1 = off[er[1]] + er[3]
    xg = _dispatch(x1, ffn_norm_w, dest0, dest1, DISPATCH_TM)
    yg = _ffn(xg, w_gate, w_up, w_down, layer, plan, FFN_TM)
    return _combine(x1, w12, yg, dest0, dest1, next_norm_w, n_p, COMBINE_TM, final)


def kernel(x_prompt, x_sample, state_ssm_conv, state_ssm, state_lru_conv, state_lru, state_sconv, norm_mix, norm_ffn, norm_final, w_ssm_in, ssm_conv_w, ssm_conv_b, ssm_dt_bias, ssm_A_log, ssm_D, ssm_norm_w, w_ssm_out, w_lru_in, lru_conv_w, lru_conv_b, lru_wa, lru_ba, lru_wi, lru_bi, lru_lambda, w_lru_out, w_sc_in, sc_conv_w, w_sc_out, w_route_group, w_route_expert, w_exp_gate, w_exp_up, w_exp_down):
    b_p, t_p, d = x_prompt.shape
    b_s, t_s, _ = x_sample.shape
    n_p = b_p * t_p
    n_s = b_s * t_s
    n = n_p + n_s
    assert n_s == ROW_TM and n_p % ROW_TM == 0 and t_p % SSM_CHUNK == 0 and t_s <= SAMPLE_PAD_T
    mm_tm = n // 4
    n_chunks_p = t_p // SSM_CHUNK
    n_a = state_ssm.shape[0]

    x_p = x_prompt.reshape(n_p, d)
    x_s = x_sample.reshape(n_s, d)
    xn = _rmsnorm(x_p, x_s, norm_mix[0], ROW_TM, BF16)

    def last_rows_p(m, col0, ncol, k):
        return jnp.stack([m[(b + 1) * t_p - k:(b + 1) * t_p, col0:col0 + ncol] for b in range(b_p)])

    def last_rows_s(state, m, col0, ncol):
        new = m[n_p:, col0:col0 + ncol].reshape(b_s, t_s, ncol)
        return jnp.concatenate([state, new], axis=1)[:, t_s:]

    ssd_h_p = ssd_h_s = None
    ssd_h0_p = jnp.zeros((n_a * b_p, SSM_INNER, SSM_STATE), F32)
    ssd_h0_s = state_ssm.reshape(n_a * b_s, SSM_INNER, SSM_STATE)
    w_ssm_in_t = jnp.swapaxes(w_ssm_in, 1, 2)

    p_ssm_conv, p_lru_conv, p_lru, p_sconv = [], [], [], []
    s_ssm_conv, s_lru_conv, s_lru, s_sconv = [], [], [], []

    for i in range(DEPTH):
        kind = i % N_MIXERS
        s = i // N_MIXERS
        w_r = jnp.pad(jnp.concatenate([w_route_group[i], w_route_expert[i]], axis=1),
                      ((0, 0), (0, LANES - MOE_GROUPS - N_EXPERTS)))
        r_hi = w_r.astype(BF16)
        r_lo = (w_r - r_hi.astype(F32)).astype(BF16)
        if kind == 0:
            z = _matmul(xn, w_ssm_in_t, s, 0, SSM_INNER, PROJ_TN, mm_tm, transposed=True)
            xbc = _matmul(xn, w_ssm_in_t, s, SSM_INNER // PROJ_TN, SSM_CONV_DIM, PROJ_TN, mm_tm, transposed=True)
            w_dt = jnp.pad(w_ssm_in_t[s, SSM_INNER + SSM_CONV_DIM:, :], ((0, LANES - SSM_HEADS), (0, 0)))[None]
            dt_raw = _matmul(xn, w_dt, 0, 0, LANES, LANES, mm_tm, transposed=True)
            params = (ssm_conv_w[s], ssm_conv_b[s].reshape(1, -1), _pad_lanes(ssm_dt_bias[s]),
                      _pad_lanes(ssm_A_log[s]), jnp.repeat(ssm_D[s], SSM_HEAD_DIM).reshape(1, -1),
                      ssm_norm_w[s].reshape(1, -1))
            act_p, ssd_h_p = _ssd_core(
                z, xbc, dt_raw, jnp.zeros((b_p, SUBLANES, SSM_CONV_DIM), F32), ssd_h0_p, *params, ssd_h_p,
                nseq=b_p, rows=SSM_CHUNK, n_chunks=n_chunks_p, t_valid=SSM_CHUNK, seq0=s * b_p, n_slots=n_a * b_p)
            act_s, ssd_h_s = _ssd_core(
                z, xbc, dt_raw, _carry_rows(state_ssm_conv[s], SSM_CONV), ssd_h0_s, *params, ssd_h_s,
                nseq=b_s, rows=SAMPLE_PAD_T, n_chunks=1, t_valid=t_s, seq0=s * b_s, n_slots=n_a * b_s,
                packed_row0=n_p)
            p_ssm_conv.append(last_rows_p(xbc, 0, SSM_CONV_DIM, SSM_CONV - 1))
            s_ssm_conv.append(last_rows_s(state_ssm_conv[s], xbc, 0, SSM_CONV_DIM))
            w_out = w_ssm_out
        elif kind == 1:
            proj = _matmul(xn, w_lru_in, s, 0, 2 * LRU_WIDTH, PROJ_TN, mm_tm)
            params = (lru_conv_w[s], lru_conv_b[s].reshape(1, -1), lru_wa[s], lru_ba[s].reshape(1, -1), lru_wi[s],
                      lru_bi[s].reshape(1, -1), lru_lambda[s].reshape(1, -1))
            act_p, h_p = _lru_core(
                proj, jnp.zeros((b_p, SUBLANES, LRU_WIDTH), F32), jnp.zeros((b_p, 1, LRU_WIDTH), F32), *params,
                nseq=b_p, rows=SSM_CHUNK, n_chunks=n_chunks_p, t_valid=SSM_CHUNK)
            act_s, h_s = _lru_core(
                proj, _carry_rows(state_lru_conv[s], LRU_CONV), state_lru[s].reshape(b_s, 1, LRU_WIDTH), *params,
                nseq=b_s, rows=SAMPLE_PAD_T, n_chunks=1, t_valid=t_s, packed_row0=n_p)
            p_lru_conv.append(last_rows_p(proj, LRU_WIDTH, LRU_WIDTH, LRU_CONV - 1))
            s_lru_conv.append(last_rows_s(state_lru_conv[s], proj, LRU_WIDTH, LRU_WIDTH))
            p_lru.append(h_p.reshape(b_p, LRU_WIDTH))
            s_lru.append(h_s.reshape(b_s, LRU_WIDTH))
            w_out = w_lru_out
        else:
            proj = _matmul(xn, w_sc_in, s, 0, 3 * d, PROJ_TN, mm_tm)
            act_p, v_p = _sc_core(proj, jnp.zeros((b_p, SUBLANES, d), F32), sc_conv_w[s],
                                  nseq=b_p, rows=SSM_CHUNK, n_chunks=n_chunks_p, t_valid=SSM_CHUNK)
            act_s, v_s = _sc_core(proj, _carry_rows(state_sconv[s], SC_WIDTH), sc_conv_w[s],
                                  nseq=b_s, rows=SAMPLE_PAD_T, n_chunks=1, t_valid=t_s, packed_row0=n_p)
            p_sconv.append(v_p[:, SUBLANES - (SC_WIDTH - 1):])
            s_sconv.append(jnp.concatenate([state_sconv[s], v_s[:, :t_s]], axis=1)[:, t_s:])
            w_out = w_sc_out

        x1, logits = _outproj(act_p, act_s, w_out[s].astype(BF16), x_p, x_s, norm_ffn[i], r_hi, r_lo, PROJ_TK)
        moe_w = (w_exp_gate, w_exp_up, w_exp_down)
        if i < DEPTH - 1:
            x_p, x_s, xn = _moe(x1, norm_ffn[i], logits, i, *moe_w, norm_mix[i + 1], n_p, False)
        else:
            y_p, y_s = _moe(x1, norm_ffn[i], logits, i, *moe_w, norm_final, n_p, True)

    state_shape = (SSM_HEADS, SSM_HEAD_DIM, SSM_STATE)
    return (y_p.reshape(b_p, t_p, d), y_s.reshape(b_s, t_s, d),
            jnp.stack(p_ssm_conv), ssd_h_p.reshape((n_a, b_p) + state_shape), jnp.stack(p_lru_conv), jnp.stack(p_lru),
            jnp.stack(p_sconv),
            jnp.stack(s_ssm_conv), ssd_h_s.reshape((n_a, b_s) + state_shape), jnp.stack(s_lru_conv), jnp.stack(s_lru),
            jnp.stack(s_sconv))
```

```python
import functools
import math

import jax
import jax.numpy as jnp
from jax import lax
from jax.experimental import pallas as pl
from jax.experimental.pallas import tpu as pltpu

F32 = jnp.float32
BF16 = jnp.bfloat16

D_MODEL = 2048
DEPTH = 4
N_MIXERS = 3
RMS_EPS = 1e-6
LOG2_E = 1.4426950408889634
SSM_INNER = 4096
SSM_HEAD_DIM = 64
SSM_HEADS = 64
SSM_GROUPS = 8
SSM_HPG = 8
SSM_STATE = 128
SSM_CONV = 4
SSM_BC = SSM_GROUPS * SSM_STATE
SSM_CONV_DIM = SSM_INNER + 2 * SSM_BC
SSM_CHUNK = 128
LRU_WIDTH = 2048
LRU_HEADS = 16
LRU_BLOCK = 128
LRU_CONV = 4
LRU_C = 8.0
SC_WIDTH = 3
MOE_GROUPS = 8
MOE_PER_GROUP = 4
N_EXPERTS = 32
MOE_FF = 512

LANES = 128
SUBLANES = 8
SAMPLE_PAD_T = 8
VMEM_LIMIT = 56 * 1024 * 1024

ROW_TM = 512
PROJ_TN = 512
PROJ_TK = 1024
ROUTE_TM = 512
DISPATCH_TM = 512
FFN_TM = 256
COMBINE_TM = 256


def _cparams(*sem):
    return pltpu.CompilerParams(dimension_semantics=sem, vmem_limit_bytes=VMEM_LIMIT)


def _silu(x):
    return x * jax.nn.sigmoid(x)


def _softplus(x):
    return jnp.maximum(x, 0.0) + jnp.log1p(jnp.exp(-jnp.abs(x)))


def _dot(a, b):
    return jnp.dot(a, b, preferred_element_type=F32)


def _dot_nt(a, b):
    return lax.dot_general(a, b, (((1,), (1,)), ((), ())), preferred_element_type=F32)


def _split3(x):
    hi = x.astype(BF16)
    r = x - hi.astype(F32)
    mid = r.astype(BF16)
    lo = (r - mid.astype(F32)).astype(BF16)
    return hi, mid, lo


def _norm_kernel(xp_ref, xs_ref, w_ref, o_ref, *, n_prompt_tiles):
    def emit(x):
        ms = jnp.mean(x * x, axis=-1, keepdims=True)
        o_ref[...] = (x * lax.rsqrt(ms + RMS_EPS) * w_ref[...]).astype(o_ref.dtype)

    @pl.when(pl.program_id(0) < n_prompt_tiles)
    def _():
        emit(xp_ref[...])

    @pl.when(pl.program_id(0) >= n_prompt_tiles)
    def _():
        emit(xs_ref[...])


def _rmsnorm(x_p, x_s, w, tm, out_dtype):
    n_p, d = x_p.shape
    n = n_p + x_s.shape[0]
    npt = n_p // tm
    return pl.pallas_call(
        functools.partial(_norm_kernel, n_prompt_tiles=npt),
        grid=(n // tm,),
        in_specs=[
            pl.BlockSpec((tm, d), lambda i: (jnp.minimum(i, npt - 1), 0)),
            pl.BlockSpec((tm, d), lambda i: (jnp.maximum(i - npt, 0), 0)),
            pl.BlockSpec((1, d), lambda i: (0, 0)),
        ],
        out_specs=pl.BlockSpec((tm, d), lambda i: (i, 0)),
        out_shape=jax.ShapeDtypeStruct((n, d), out_dtype),
        compiler_params=_cparams("arbitrary"),
        name="rmsnorm",
    )(x_p, x_s, w.reshape(1, d))


def _mm_kernel(a_ref, w_ref, o_ref, *, transposed):
    w = w_ref[...].astype(BF16)
    o_ref[...] = _dot_nt(a_ref[...], w) if transposed else _dot(a_ref[...], w)


def _matmul(a, w_stack, layer, col_blk0, ncols, tn, tm, transposed=False):
    n, k = a.shape
    if transposed:
        w_spec = pl.BlockSpec((None, tn, k), lambda i, j: (layer, col_blk0 + j, 0))
    else:
        w_spec = pl.BlockSpec((None, k, tn), lambda i, j: (layer, 0, col_blk0 + j))
    return pl.pallas_call(
        functools.partial(_mm_kernel, transposed=transposed),
        grid=(n // tm, ncols // tn),
        in_specs=[pl.BlockSpec((tm, k), lambda i, j: (i, 0)), w_spec],
        out_specs=pl.BlockSpec((tm, tn), lambda i, j: (i, j)),
        out_shape=jax.ShapeDtypeStruct((n, ncols), F32),
        compiler_params=_cparams("parallel", "arbitrary"),
        name="in_proj",
    )(a, w_stack)


def _outproj_kernel(ap_ref, as_ref, w_ref, xp_ref, xs_ref, nw_ref, rh_ref, rl_ref, x1_ref, lg_ref, acc_ref,
                    *, n_prompt_tiles, nk, tk):
    i = pl.program_id(0)
    k = pl.program_id(1)
    w = w_ref[pl.ds(pl.multiple_of(k * tk, tk), tk), :]

    @pl.when(i < n_prompt_tiles)
    def _():
        @pl.when(k == 0)
        def _():
            acc_ref[...] = xp_ref[...]

        acc_ref[...] += _dot(ap_ref[...], w)

    @pl.when(i >= n_prompt_tiles)
    def _():
        @pl.when(k == 0)
        def _():
            acc_ref[...] = xs_ref[...]

        acc_ref[...] += _dot(as_ref[...].astype(BF16), w)

    @pl.when(k == nk - 1)
    def _():
        x1 = acc_ref[...]
        x1_ref[...] = x1
        ms = jnp.mean(x1 * x1, axis=-1, keepdims=True)
        xn = x1 * lax.rsqrt(ms + RMS_EPS) * nw_ref[...]
        hi = xn.astype(BF16)
        lo = (xn - hi.astype(F32)).astype(BF16)
        rh = rh_ref[...]
        lg_ref[...] = _dot(hi, rh) + (_dot(hi, rl_ref[...]) + _dot(lo, rh))


def _outproj(act_p, act_s, w_bf16, x_p, x_s, norm_w, r_hi, r_lo, tk):
    n_p, kdim = act_p.shape
    d = x_p.shape[1]
    tm = act_s.shape[0]
    n = n_p + tm
    npt = n_p // tm
    nk = kdim // tk
    kern = functools.partial(_outproj_kernel, n_prompt_tiles=npt, nk=nk, tk=tk)
    return pl.pallas_call(
        kern,
        grid=(n // tm, nk),
        in_specs=[
            pl.BlockSpec((tm, tk), lambda i, k: (jnp.minimum(i, npt - 1), jnp.where(i < npt, k, nk - 1))),
            pl.BlockSpec((tm, tk), lambda i, k: (0, jnp.where(i >= npt, k, 0))),
            pl.BlockSpec((kdim, d), lambda i, k: (0, 0), pipeline_mode=pl.Buffered(1)),
            pl.BlockSpec((tm, d), lambda i, k: (jnp.minimum(i, npt - 1), 0)),
            pl.BlockSpec((tm, d), lambda i, k: (0, 0), pipeline_mode=pl.Buffered(1)),
            pl.BlockSpec((1, d), lambda i, k: (0, 0)),
            pl.BlockSpec((d, LANES), lambda i, k: (0, 0)),
            pl.BlockSpec((d, LANES), lambda i, k: (0, 0)),
        ],
        out_specs=[
            pl.BlockSpec((tm, d), lambda i, k: (i, 0)),
            pl.BlockSpec((tm, LANES), lambda i, k: (i, 0)),
        ],
        out_shape=[
            jax.ShapeDtypeStruct((n, d), F32),
            jax.ShapeDtypeStruct((n, LANES), F32),
        ],
        scratch_shapes=[pltpu.VMEM((tm, d), F32)],
        compiler_params=_cparams("arbitrary", "arbitrary"),
        name="out_proj",
    )(act_p, act_s, w_bf16, x_p, x_s, norm_w.reshape(1, d), r_hi, r_lo)


def _causal_conv(carry_ref, x, w_ref, bias, carry0_ref, width, rows, first):
    @pl.when(first)
    def _():
        carry_ref[...] = carry0_ref[...]

    n_tiles = rows // SUBLANES
    tiles = [carry_ref[...]] + [x[i * SUBLANES:(i + 1) * SUBLANES, :] for i in range(n_tiles)]
    sub = lax.broadcasted_iota(jnp.int32, tiles[0].shape, 0)
    y = None
    for k in range(width):
        back = width - 1 - k
        if back == 0:
            tap = x
        else:
            rolled = [pltpu.roll(t, back, 0) for t in tiles]
            tap = jnp.concatenate(
                [jnp.where(sub < back, rolled[i], rolled[i + 1]) for i in range(n_tiles)], axis=0)
        term = tap * w_ref[k:k + 1, :]
        y = term if y is None else y + term
    if bias is not None:
        y = y + bias
    carry_ref[...] = tiles[-1]
    return y


def _pad_rows(a, rows):
    if a.shape[0] == rows:
        return a
    return jnp.concatenate([a, jnp.zeros((rows - a.shape[0], a.shape[1]), a.dtype)], axis=0)


def _loader(packed, t_valid, second=None):
    if not packed:
        return lambda ref: ref[...]
    if second is None:
        second = lax.rem(pl.program_id(0), 2) == 1

    def load(ref):
        x8 = ref[...]
        if isinstance(second, bool):
            x8 = pltpu.roll(x8, SUBLANES - t_valid, 0) if second else x8
        else:
            x8 = jnp.where(second, pltpu.roll(x8, SUBLANES - t_valid, 0), x8)
        sub = lax.broadcasted_iota(jnp.int32, x8.shape, 0)
        x8 = jnp.where(sub < t_valid, x8, 0.0)
        return _pad_rows(x8, SAMPLE_PAD_T)

    return load


def _row_blocks(nseq, rows, n_chunks, t_valid, packed_row0):
    if packed_row0 is None:
        last = nseq * n_chunks - 1
        rmap = lambda b, c: (jnp.where(b < nseq, b * n_chunks + c, last), 0)
        return rows, rmap, rmap, nseq * n_chunks * rows, BF16
    assert n_chunks == 1 and 2 * t_valid == SUBLANES and nseq % 2 == 0 and packed_row0 % SUBLANES == 0
    blk0 = packed_row0 // SUBLANES
    imap = lambda b, c: (blk0 + jnp.minimum(b, nseq - 1) // 2, 0)
    omap = lambda b, c: (jnp.minimum(b, nseq - 1) // 2, 0)
    return SUBLANES, imap, omap, nseq * t_valid, F32


def _store_act(act_ref, act, packed, t_valid):
    if not packed:
        act_ref[...] = act.astype(act_ref.dtype)
        return
    second = lax.rem(pl.program_id(0), 2) == 1
    a8 = act[0:SUBLANES, :].astype(act_ref.dtype)

    @pl.when(jnp.logical_not(second))
    def _():
        act_ref[...] = a8

    @pl.when(second)
    def _():
        sub = lax.broadcasted_iota(jnp.int32, a8.shape, 0)
        act_ref[...] = jnp.where(sub < t_valid, act_ref[...], pltpu.roll(a8, t_valid, 0))


def _ssd_kernel(*refs, n_steps, n_fill, aliased, **static):
    if aliased:
        refs = refs[:11] + refs[12:]
    data, params, act_ref, (ht_ref, xp_ref, h_ref) = refs[:5], refs[5:11], refs[11], refs[12:]
    t_valid = static["t_valid"]

    def run():
        if not static["packed"]:
            act = _ssd_body(*data, *params, ht_ref, xp_ref, h_ref, second=None, **static)
            act_ref[...] = act.astype(act_ref.dtype)
            return
        z_ref, xbc_ref, dt_ref, carry0_ref, h0_ref = data
        acts = [_ssd_body(z_ref, xbc_ref, dt_ref, carry0_ref.at[q], h0_ref.at[q], *params, ht_ref.at[q],
                          xp_ref.at[q], h_ref, second=bool(q), **static)[0:SUBLANES, :] for q in (0, 1)]
        sub = lax.broadcasted_iota(jnp.int32, acts[0].shape, 0)
        act_ref[...] = jnp.where(sub < t_valid, acts[0], pltpu.roll(acts[1], t_valid, 0)).astype(act_ref.dtype)

    if n_fill == 0:
        run()
        return

    @pl.when(pl.program_id(0) < n_steps)
    def _():
        run()

    @pl.when(pl.program_id(0) >= n_steps)
    def _():
        ht_ref[...] = jnp.zeros_like(ht_ref)


def _ssd_body(z_ref, xbc_ref, dt_ref, carry0_ref, h0_ref, cw_ref, cb_ref, dtb_ref, alog_ref, dexp_ref, nw_ref,
              ht_ref, xp_ref, h_ref, *, rows, n_chunks, t_valid, packed, second):
    c = pl.program_id(1)
    lp = SSM_CHUNK
    load = _loader(packed, t_valid, second)
    if n_chunks == 1:
        h_in, h_out = h0_ref, ht_ref
    else:
        h_in = h_out = h_ref

        @pl.when(c == 0)
        def _():
            h_ref[...] = h0_ref[...]

    conv = _causal_conv(xp_ref, load(xbc_ref), cw_ref, cb_ref[...], carry0_ref, SSM_CONV, rows, c == 0)
    xbc = _silu(conv)
    xs = xbc[:, :SSM_INNER]
    bm = xbc[:, SSM_INNER:SSM_INNER + SSM_BC]
    cm = xbc[:, SSM_INNER + SSM_BC:]

    row_q = lax.broadcasted_iota(jnp.int32, (rows, LANES), 0)
    dt = _softplus(load(dt_ref) + dtb_ref[...])
    if t_valid < rows:
        dt = jnp.where(row_q < t_valid, dt, 0.0)
    a_neg = -jnp.exp(alog_ref[...])
    dt_p = _pad_rows(dt, lp)
    ad = dt_p * a_neg
    r_i = lax.broadcasted_iota(jnp.int32, (lp, lp), 0)
    c_i = lax.broadcasted_iota(jnp.int32, (lp, lp), 1)
    tri = jnp.where(c_i <= r_i, 1.0, 0.0).astype(BF16)
    a_hi, a_mid, a_lo = _split3(ad)
    cs = _dot(tri, a_hi) + (_dot(tri, a_mid) + _dot(tri, a_lo))
    cs2 = cs * LOG2_E
    cs_t = cs2.T
    dt_t = dt_p.T
    cs_last = cs[lp - 1:lp, :]
    to_end = jnp.exp(cs_last - cs)
    w_t = (dt_p * to_end).T
    e_last_t = jnp.broadcast_to(jnp.exp(cs_last), (lp, LANES)).T
    cs_q = cs2[:rows, :]
    lane_q = lax.broadcasted_iota(jnp.int32, (rows, LANES), 1)
    causal = lax.broadcasted_iota(jnp.int32, (rows, lp), 1) <= lax.broadcasted_iota(jnp.int32, (rows, lp), 0)
    lane_lo = lane_q < SSM_HEAD_DIM

    y_groups = []
    for g in range(SSM_GROUPS):
        cg = cm[:, g * SSM_STATE:(g + 1) * SSM_STATE].astype(BF16)
        bg = _pad_rows(bm[:, g * SSM_STATE:(g + 1) * SSM_STATE], lp).astype(BF16)
        gw = SSM_HPG * SSM_HEAD_DIM
        hg = h_in[g * gw:(g + 1) * gw, :]
        cb = _dot_nt(cg, bg)
        yoff = _dot_nt(cg, hg.astype(BF16))
        xs_g = _pad_rows(xs[:, g * gw:(g + 1) * gw], lp)
        y_pairs = []
        for j in range(SSM_HPG // 2):
            ha = g * SSM_HPG + 2 * j
            hb = ha + 1
            col_a = jnp.sum(jnp.where(lane_q == ha, cs_q, 0.0), axis=1, keepdims=True)
            col_b = jnp.sum(jnp.where(lane_q == hb, cs_q, 0.0), axis=1, keepdims=True)
            m_a = cb * jnp.exp2(jnp.where(causal, col_a - cs_t[ha:ha + 1, :], -jnp.inf)) * dt_t[ha:ha + 1, :]
            m_b = cb * jnp.exp2(jnp.where(causal, col_b - cs_t[hb:hb + 1, :], -jnp.inf)) * dt_t[hb:hb + 1, :]
            xs_pair = xs_g[:, j * LANES:(j + 1) * LANES].astype(BF16)
            ydiag = jnp.where(lane_lo, _dot(m_a.astype(BF16), xs_pair), _dot(m_b.astype(BF16), xs_pair))
            scale = jnp.where(lane_lo, jnp.exp2(col_a), jnp.exp2(col_b))
            y_pairs.append(ydiag + yoff[:, j * LANES:(j + 1) * LANES] * scale)
        y_groups.append(jnp.concatenate(y_pairs, axis=1))
        xs_t = xs_g.T
        w_rows = jnp.concatenate(
            [jnp.broadcast_to(w_t[g * SSM_HPG + j:g * SSM_HPG + j + 1, :], (SSM_HEAD_DIM, lp)) for j in range(SSM_HPG)],
            axis=0)
        d_rows = jnp.concatenate(
            [jnp.broadcast_to(e_last_t[g * SSM_HPG + j:g * SSM_HPG + j + 1, :], (SSM_HEAD_DIM, LANES))
             for j in range(SSM_HPG)], axis=0)
        st = _dot((xs_t * w_rows).astype(BF16), bg)
        h_out[g * gw:(g + 1) * gw, :] = hg * d_rows + st

    y = jnp.concatenate(y_groups, axis=1)
    y = y + xs * dexp_ref[...]
    y = y * _silu(load(z_ref))
    gsz = SSM_INNER // SSM_GROUPS
    outs = []
    for g in range(SSM_GROUPS):
        yg = y[:, g * gsz:(g + 1) * gsz]
        ms = jnp.mean(yg * yg, axis=-1, keepdims=True)
        outs.append(yg * lax.rsqrt(ms + RMS_EPS))
    if n_chunks > 1:
        @pl.when(c == n_chunks - 1)
        def _():
            ht_ref[...] = h_ref[...]

    return jnp.concatenate(outs, axis=1) * nw_ref[...]


def _ssd_core(z, xbc, dt_raw, carry0, h0, conv_w, conv_b, dt_bias, a_log, d_exp, norm_w, ht_buf, *, nseq, rows,
              n_chunks, t_valid, seq0, n_slots, packed_row0=None):
    aliased = ht_buf is not None
    packed = packed_row0 is not None
    per_step = 2 if packed else 1
    n_fill_slots = 0 if aliased else n_slots - seq0 - nseq
    assert nseq % per_step == 0 and seq0 % per_step == 0 and n_fill_slots % per_step == 0
    n_steps, n_fill, step0 = nseq // per_step, n_fill_slots // per_step, seq0 // per_step
    kern = functools.partial(_ssd_kernel, n_steps=n_steps, n_fill=n_fill, aliased=aliased, rows=rows,
                             n_chunks=n_chunks, t_valid=t_valid, packed=packed)
    seq_dim = 2 if packed else None
    if packed:
        assert n_chunks == 1 and 2 * t_valid == SUBLANES and packed_row0 % SUBLANES == 0
        blk_rows, act_rows, act_dtype = SUBLANES, nseq * t_valid, F32
        rmap = lambda b, c: (packed_row0 // SUBLANES + jnp.minimum(b, n_steps - 1), 0)
        omap = lambda b, c: (jnp.minimum(b, n_steps - 1), 0)
    else:
        blk_rows, rmap, omap, act_rows, act_dtype = _row_blocks(nseq, rows, n_chunks, t_valid, None)
    smap = lambda b, c: (jnp.minimum(b, n_steps - 1), 0, 0)
    const = lambda b, c: (0, 0)
    in_specs = [
        pl.BlockSpec((blk_rows, SSM_INNER), rmap),
        pl.BlockSpec((blk_rows, SSM_CONV_DIM), rmap),
        pl.BlockSpec((blk_rows, LANES), rmap),
        pl.BlockSpec((seq_dim, SUBLANES, SSM_CONV_DIM), smap),
        pl.BlockSpec((seq_dim, SSM_INNER, SSM_STATE), lambda b, c: (step0 + jnp.minimum(b, n_steps - 1), 0, 0)),
        pl.BlockSpec((SSM_CONV, SSM_CONV_DIM), const),
        pl.BlockSpec((1, SSM_CONV_DIM), const),
        pl.BlockSpec((1, LANES), const),
        pl.BlockSpec((1, LANES), const),
        pl.BlockSpec((1, SSM_INNER), const),
        pl.BlockSpec((1, SSM_INNER), const),
    ]
    args = [z, xbc, dt_raw, carry0, h0, conv_w, conv_b, dt_bias, a_log, d_exp, norm_w]
    if aliased:
        in_specs.append(pl.BlockSpec(memory_space=pl.ANY))
        args.append(ht_buf)
    return pl.pallas_call(
        kern,
        grid=(n_steps + n_fill, n_chunks),
        in_specs=in_specs,
        out_specs=[
            pl.BlockSpec((blk_rows, SSM_INNER), omap),
            pl.BlockSpec((seq_dim, SSM_INNER, SSM_STATE), lambda b, c: (step0 + b, 0, 0)),
        ],
        out_shape=[
            jax.ShapeDtypeStruct((act_rows, SSM_INNER), act_dtype),
            jax.ShapeDtypeStruct((n_slots, SSM_INNER, SSM_STATE), F32),
        ],
        scratch_shapes=[
            pltpu.VMEM((2, SUBLANES, SSM_CONV_DIM) if packed else (SUBLANES, SSM_CONV_DIM), F32),
            pltpu.VMEM((SSM_INNER, SSM_STATE), F32),
        ],
        input_output_aliases={11: 1} if aliased else {},
        compiler_params=_cparams("arbitrary", "arbitrary"),
        name="ssd_core",
    )(*args)


def _lru_kernel(gin_ref, xr_ref, carry0_ref, h0_ref, cw_ref, cb_ref, wa_ref, ba_ref, wi_ref, bi_ref, lam_ref,
                act_ref, ht_ref, xp_ref, hc_ref, *, rows, n_chunks, t_valid, packed):
    c = pl.program_id(1)
    load = _loader(packed, t_valid)

    @pl.when(c == 0)
    def _():
        hc_ref[...] = h0_ref[...]

    xc = _causal_conv(xp_ref, load(xr_ref), cw_ref, cb_ref[...], carry0_ref, LRU_CONV, rows, c == 0)
    ra = []
    ia = []
    for h in range(LRU_HEADS):
        xh = xc[:, h * LRU_BLOCK:(h + 1) * LRU_BLOCK].astype(BF16)
        ra.append(_dot(xh, wa_ref[h].astype(BF16)))
        ia.append(_dot(xh, wi_ref[h].astype(BF16)))
    r = jax.nn.sigmoid(jnp.concatenate(ra, axis=1) + ba_ref[...])
    ig = jax.nn.sigmoid(jnp.concatenate(ia, axis=1) + bi_ref[...])
    log_a = -LRU_C * r * _softplus(-lam_ref[...])
    a = jnp.exp(log_a)
    th = jnp.tanh(log_a)
    mult = jnp.sqrt(-2.0 * th / (1.0 - th))
    u = mult * (ig * xc)
    if t_valid < rows:
        row = lax.broadcasted_iota(jnp.int32, (rows, LRU_WIDTH), 0)
        a = jnp.where(row < t_valid, a, 1.0)
        u = jnp.where(row < t_valid, u, 0.0)

    sub = lax.broadcasted_iota(jnp.int32, (SUBLANES, LRU_WIDTH), 0)
    h_prev = hc_ref[...]
    h_blocks = []
    for blk in range(rows // SUBLANES):
        ab = a[blk * SUBLANES:(blk + 1) * SUBLANES, :]
        ub = u[blk * SUBLANES:(blk + 1) * SUBLANES, :]
        for d in (1, 2, 4):
            a_sh = jnp.where(sub >= d, pltpu.roll(ab, d, 0), 1.0)
            u_sh = jnp.where(sub >= d, pltpu.roll(ub, d, 0), 0.0)
            ub = ub + ab * u_sh
            ab = ab * a_sh
        hb = ub + ab * h_prev
        h_blocks.append(hb)
        h_prev = hb[SUBLANES - 1:SUBLANES, :]
    hc_ref[...] = h_prev
    hseq = jnp.concatenate(h_blocks, axis=0)
    gx = load(gin_ref)
    gate = 0.5 * gx * (1.0 + jnp.tanh(math.sqrt(2.0 / math.pi) * (gx + 0.044715 * (gx * gx * gx))))
    _store_act(act_ref, hseq * gate, packed, t_valid)

    @pl.when(c == n_chunks - 1)
    def _():
        ht_ref[...] = hc_ref[...]


def _lru_core(proj, carry0, h0, conv_w, conv_b, wa, ba, wi, bi, lam, *, nseq, rows, n_chunks, t_valid,
              packed_row0=None):
    kern = functools.partial(_lru_kernel, rows=rows, n_chunks=n_chunks, t_valid=t_valid,
                             packed=packed_row0 is not None)
    blk_rows, rmap, omap, act_rows, act_dtype = _row_blocks(nseq, rows, n_chunks, t_valid, packed_row0)
    w = LRU_WIDTH
    const = lambda b, c: (0, 0)
    const3 = lambda b, c: (0, 0, 0)
    return pl.pallas_call(
        kern,
        grid=(nseq, n_chunks),
        in_specs=[
            pl.BlockSpec((blk_rows, w), rmap),
            pl.BlockSpec((blk_rows, w), lambda b, c: (rmap(b, c)[0], 1)),
            pl.BlockSpec((None, SUBLANES, w), lambda b, c: (b, 0, 0)),
            pl.BlockSpec((None, 1, w), lambda b, c: (b, 0, 0)),
            pl.BlockSpec((LRU_CONV, w), const),
            pl.BlockSpec((1, w), const),
            pl.BlockSpec((LRU_HEADS, LRU_BLOCK, LRU_BLOCK), const3),
            pl.BlockSpec((1, w), const),
            pl.BlockSpec((LRU_HEADS, LRU_BLOCK, LRU_BLOCK), const3),
            pl.BlockSpec((1, w), const),
            pl.BlockSpec((1, w), const),
        ],
        out_specs=[
            pl.BlockSpec((blk_rows, w), omap),
            pl.BlockSpec((None, 1, w), lambda b, c: (b, 0, 0)),
        ],
        out_shape=[
            jax.ShapeDtypeStruct((act_rows, w), act_dtype),
            jax.ShapeDtypeStruct((nseq, 1, w), F32),
        ],
        scratch_shapes=[pltpu.VMEM((SUBLANES, w), F32), pltpu.VMEM((1, w), F32)],
        compiler_params=_cparams("arbitrary", "arbitrary"),
        name="lru_core",
    )(proj, proj, carry0, h0, conv_w, conv_b, wa, ba, wi, bi, lam)


def _sc_kernel(bg_ref, cg_ref, hh_ref, carry0_ref, cw_ref, act_ref, vt_ref, xp_ref, *, rows, n_chunks, t_valid,
               packed):
    c = pl.program_id(1)
    load = _loader(packed, t_valid)
    v = load(cg_ref) * load(hh_ref)
    u = _causal_conv(xp_ref, v, cw_ref, None, carry0_ref, SC_WIDTH, rows, c == 0)
    _store_act(act_ref, load(bg_ref) * u, packed, t_valid)
    lo = max(t_valid - SUBLANES, 0)

    @pl.when(c == n_chunks - 1)
    def _():
        vt_ref[...] = v[lo:lo + SUBLANES, :]


def _sc_core(proj, carry0, conv_w, *, nseq, rows, n_chunks, t_valid, packed_row0=None):
    kern = functools.partial(_sc_kernel, rows=rows, n_chunks=n_chunks, t_valid=t_valid,
                             packed=packed_row0 is not None)
    blk_rows, rmap, omap, act_rows, act_dtype = _row_blocks(nseq, rows, n_chunks, t_valid, packed_row0)
    d = D_MODEL
    return pl.pallas_call(
        kern,
        grid=(nseq, n_chunks),
        in_specs=[
            pl.BlockSpec((blk_rows, d), rmap),
            pl.BlockSpec((blk_rows, d), lambda b, c: (rmap(b, c)[0], 1)),
            pl.BlockSpec((blk_rows, d), lambda b, c: (rmap(b, c)[0], 2)),
            pl.BlockSpec((None, SUBLANES, d), lambda b, c: (b, 0, 0)),
            pl.BlockSpec((SC_WIDTH, d), lambda b, c: (0, 0)),
        ],
        out_specs=[
            pl.BlockSpec((blk_rows, d), omap),
            pl.BlockSpec((None, SUBLANES, d), lambda b, c: (b, 0, 0)),
        ],
        out_shape=[
            jax.ShapeDtypeStruct((act_rows, d), act_dtype),
            jax.ShapeDtypeStruct((nseq, SUBLANES, d), F32),
        ],
        scratch_shapes=[pltpu.VMEM((SUBLANES, d), F32)],
        compiler_params=_cparams("arbitrary", "arbitrary"),
        name="sconv_core",
    )(proj, proj, proj, carry0, conv_w)


def _route_kernel(lg_ref, info_ref, w12_ref, cnt_ref, carry_ref, *, tm, n_tiles):
    i = pl.program_id(0)

    @pl.when(i == 0)
    def _():
        carry_ref[...] = jnp.zeros_like(carry_ref)

    lg = lg_ref[...]
    lane = lax.broadcasted_iota(jnp.int32, (tm, LANES), 1).astype(F32)
    neg = -jnp.inf
    big = float(LANES)
    is_g = lane < MOE_GROUPS
    gl = jnp.where(is_g, lg, neg)
    gmax = jnp.max(gl, axis=1, keepdims=True)
    gidx = jnp.min(jnp.where(gl == gmax, lane, big), axis=1, keepdims=True)
    gsum = jnp.sum(jnp.where(is_g, jnp.exp(lg - gmax), 0.0), axis=1, keepdims=True)
    gw = 1.0 / gsum
    rel = lane - (MOE_GROUPS + MOE_PER_GROUP * gidx)
    in_grp = (rel >= 0.0) & (rel < MOE_PER_GROUP)
    el = jnp.where(in_grp, lg, neg)
    v1 = jnp.max(el, axis=1, keepdims=True)
    i1 = jnp.min(jnp.where(in_grp & (el == v1), lane, big), axis=1, keepdims=True)
    in_grp2 = in_grp & (lane != i1)
    el2 = jnp.where(in_grp2, lg, neg)
    v2 = jnp.max(el2, axis=1, keepdims=True)
    i2 = jnp.min(jnp.where(in_grp2 & (el2 == v2), lane, big), axis=1, keepdims=True)
    t = jnp.exp(v2 - v1)
    den = 1.0 + t
    w1 = (1.0 / den) * gw
    w2 = (t / den) * gw
    e1 = i1 - MOE_GROUPS
    e2 = i2 - MOE_GROUPS
    onehot = jnp.where(lane == e1, 1.0, jnp.where(lane == e2, 1.0, 0.0))
    r_i = lax.broadcasted_iota(jnp.int32, (tm, tm), 0)
    c_i = lax.broadcasted_iota(jnp.int32, (tm, tm), 1)
    tri = jnp.where(c_i < r_i, 1.0, 0.0).astype(BF16)
    before = _dot(tri, onehot.astype(BF16)) + carry_ref[0:1, :]
    r1 = jnp.sum(jnp.where(lane == e1, before, 0.0), axis=1, keepdims=True)
    r2 = jnp.sum(jnp.where(lane == e2, before, 0.0), axis=1, keepdims=True)
    carry_ref[0:1, :] = carry_ref[0:1, :] + jnp.sum(onehot, axis=0, keepdims=True)
    info_ref[...] = jnp.where(lane == 0, e1, jnp.where(lane == 1, e2, jnp.where(lane == 2, r1, jnp.where(
        lane == 3, r2, 0.0))))
    w12_ref[...] = jnp.concatenate(
        [jnp.broadcast_to(w1, (tm, LANES)), jnp.broadcast_to(w2, (tm, LANES))], axis=1)

    @pl.when(i == n_tiles - 1)
    def _():
        cnt_ref[...] = carry_ref[...]


def _route(logits, tm):
    n = logits.shape[0]
    n_tiles = n // tm
    kern = functools.partial(_route_kernel, tm=tm, n_tiles=n_tiles)
    return pl.pallas_call(
        kern,
        grid=(n_tiles,),
        in_specs=[pl.BlockSpec((tm, LANES), lambda i: (i, 0))],
        out_specs=[
            pl.BlockSpec((tm, LANES), lambda i: (i, 0)),
            pl.BlockSpec((tm, 2 * LANES), lambda i: (i, 0)),
            pl.BlockSpec((SUBLANES, LANES), lambda i: (0, 0)),
        ],
        out_shape=[
            jax.ShapeDtypeStruct((n, LANES), F32),
            jax.ShapeDtypeStruct((n, 2 * LANES), F32),
            jax.ShapeDtypeStruct((SUBLANES, LANES), F32),
        ],
        scratch_shapes=[pltpu.VMEM((SUBLANES, LANES), F32)],
        compiler_params=_cparams("arbitrary"),
        name="route",
    )(logits)


def _dispatch_kernel(d0_ref, d1_ref, x_ref, nw_ref, xg_ref, xn_ref, sem, *, tm):
    base = pl.program_id(0) * tm
    x = x_ref[...]
    ms = jnp.mean(x * x, axis=-1, keepdims=True)
    xn_ref[...] = x * lax.rsqrt(ms + RMS_EPS) * nw_ref[...]

    def row_copy(r, d):
        return pltpu.make_async_copy(xn_ref.at[pl.ds(r, 1), :], xg_ref.at[pl.ds(d, 1), :], sem)

    def issue(g, carry):
        r0 = pl.multiple_of(g * SUBLANES, SUBLANES)
        for j in range(SUBLANES):
            row_copy(r0 + j, d0_ref[base + r0 + j]).start(priority=0)
            row_copy(r0 + j, d1_ref[base + r0 + j]).start(priority=1)
        return carry

    lax.fori_loop(0, tm // SUBLANES, issue, 0)

    def drain(g, carry):
        for _ in range(2 * SUBLANES):
            row_copy(0, 0).wait()
        return carry

    lax.fori_loop(0, tm // SUBLANES, drain, 0)


def _dispatch(x1, norm_w, dest0, dest1, tm):
    n, d = x1.shape
    grid_spec = pltpu.PrefetchScalarGridSpec(
        num_scalar_prefetch=2,
        grid=(n // tm,),
        in_specs=[pl.BlockSpec((tm, d), lambda i, *_: (i, 0)), pl.BlockSpec((1, d), lambda i, *_: (0, 0))],
        out_specs=pl.BlockSpec(memory_space=pl.ANY),
        scratch_shapes=[pltpu.VMEM((tm, d), F32), pltpu.SemaphoreType.DMA(())],
    )
    return pl.pallas_call(
        functools.partial(_dispatch_kernel, tm=tm),
        grid_spec=grid_spec,
        out_shape=jax.ShapeDtypeStruct((2 * n, d), F32),
        compiler_params=_cparams("arbitrary"),
        name="moe_dispatch",
    )(dest0, dest1, x1, norm_w.reshape(1, d))


VISIT_VALID = 1
VISIT_FIRST_OF_EXPERT = 2
VISIT_SLOT = 4


def _ffn_kernel(vt_ref, ve_ref, vf_ref, vn_ref, off_ref, x_ref, wg_hbm, wu_hbm, wd_hbm, o_ref,
                sg_ref, su_ref, sd_ref, cg_ref, cu_ref, cd_ref, sem, *, tm, layer):
    v = pl.program_id(0)
    t = vt_ref[v]
    e = ve_ref[v]
    flags = vf_ref[v]
    first_of_tile = jnp.logical_or(v == 0, vt_ref[jnp.maximum(v - 1, 0)] != t)

    @pl.when(first_of_tile)
    def _():
        o_ref[...] = jnp.zeros_like(o_ref)

    def copies(expert, slot):
        return (pltpu.make_async_copy(wg_hbm.at[layer, expert], sg_ref.at[slot], sem.at[slot, 0]),
                pltpu.make_async_copy(wu_hbm.at[layer, expert], su_ref.at[slot], sem.at[slot, 1]),
                pltpu.make_async_copy(wd_hbm.at[layer, expert], sd_ref.at[slot], sem.at[slot, 2]))

    @pl.when((flags & VISIT_FIRST_OF_EXPERT) != 0)
    def _():
        slot = (flags & VISIT_SLOT) // VISIT_SLOT
        nxt = vn_ref[v]

        @pl.when(v == 0)
        def _():
            for c in copies(e, slot):
                c.start()

        @pl.when(nxt >= 0)
        def _():
            for c in copies(nxt, 1 - slot):
                c.start()

        for c in copies(e, slot):
            c.wait()
        cg_ref[...] = sg_ref[slot].astype(BF16)
        cu_ref[...] = su_ref[slot].astype(BF16)
        cd_ref[...] = sd_ref[slot].astype(BF16)

    @pl.when((flags & VISIT_VALID) != 0)
    def _():
        row = t * tm + lax.broadcasted_iota(jnp.int32, (tm, MOE_FF), 0)
        mine = (row >= off_ref[e]) & (row < off_ref[e + 1])
        x = x_ref[...].astype(BF16)
        hg = _dot(x, cg_ref[...])
        hu = _dot(x, cu_ref[...])
        hid = jnp.where(mine, _silu(hg) * hu, 0.0)
        o_ref[...] += _dot(hid.astype(BF16), cd_ref[...])


def _ffn(xg, w_gate, w_up, w_down, layer, plan, tm):
    rows, d = xg.shape
    n_visits = plan[0].shape[0]
    kern = functools.partial(_ffn_kernel, tm=tm, layer=layer)
    grid_spec = pltpu.PrefetchScalarGridSpec(
        num_scalar_prefetch=5,
        grid=(n_visits,),
        in_specs=[
            pl.BlockSpec((tm, d), lambda v, vt, *_: (vt[v], 0)),
            pl.BlockSpec(memory_space=pl.ANY),
            pl.BlockSpec(memory_space=pl.ANY),
            pl.BlockSpec(memory_space=pl.ANY),
        ],
        out_specs=pl.BlockSpec((tm, d), lambda v, vt, *_: (vt[v], 0)),
        scratch_shapes=[
            pltpu.VMEM((2, d, MOE_FF), F32), pltpu.VMEM((2, d, MOE_FF), F32), pltpu.VMEM((2, MOE_FF, d), F32),
            pltpu.VMEM((d, MOE_FF), BF16), pltpu.VMEM((d, MOE_FF), BF16), pltpu.VMEM((MOE_FF, d), BF16),
            pltpu.SemaphoreType.DMA((2, 3)),
        ],
    )
    return pl.pallas_call(
        kern,
        grid_spec=grid_spec,
        out_shape=jax.ShapeDtypeStruct((rows, d), F32),
        compiler_params=_cparams("arbitrary"),
        name="moe_ffn",
    )(*plan, xg, w_gate, w_up, w_down)


def _visit_plan(counts, n_tiles, tm):
    off = jnp.concatenate([jnp.zeros((1,), jnp.int32), jnp.cumsum(counts)])
    n_visits = n_tiles + N_EXPERTS - 1
    first_tile = off[:-1] // tm
    last_tile = jnp.where(counts > 0, (off[1:] - 1) // tm, first_tile)
    n_e = jnp.where(counts > 0, last_tile - first_tile + 1, 0)
    v_off = jnp.concatenate([jnp.zeros((1,), jnp.int32), jnp.cumsum(n_e)])
    total = v_off[-1]
    vid = jnp.arange(n_visits, dtype=jnp.int32)
    exp_of = jnp.sum((vid[:, None] >= v_off[None, 1:]).astype(jnp.int32), axis=1)
    exp_of = jnp.minimum(exp_of, N_EXPERTS - 1)
    tile_of = first_tile[exp_of] + (vid - v_off[exp_of])
    valid = vid < total
    first_of_expert = valid & (vid == v_off[exp_of])
    used = counts > 0
    order = jnp.cumsum(used.astype(jnp.int32)) - 1
    ids = jnp.arange(N_EXPERTS, dtype=jnp.int32)
    later = jnp.where(used[None, :] & (ids[None, :] > ids[:, None]), ids[None, :], N_EXPERTS)
    next_used = jnp.min(later, axis=1)
    next_used = jnp.where(next_used < N_EXPERTS, next_used, -1)
    last_e = jnp.minimum(jnp.sum((total - 1 >= v_off[1:]).astype(jnp.int32)), N_EXPERTS - 1)
    exp_of = jnp.where(valid, exp_of, last_e)
    tile_of = jnp.where(valid, tile_of, n_tiles - 1)
    flags = valid * VISIT_VALID + first_of_expert * VISIT_FIRST_OF_EXPERT + (order[exp_of] % 2) * VISIT_SLOT
    i32 = lambda a: a.astype(jnp.int32)
    return i32(tile_of), i32(exp_of), i32(flags), i32(next_used[exp_of]), i32(off)


def _combine_kernel(d0_ref, d1_ref, x1_ref, w12_ref, nw_ref, yg_ref, *refs, tm, n_tiles, n_prompt_tiles, final):
    outs, (buf, sem) = refs[:-2], refs[-2:]
    i = pl.program_id(0)

    def row_copy(slot, k, r, d):
        return pltpu.make_async_copy(yg_ref.at[pl.ds(d, 1), :], buf.at[slot, k, pl.ds(r, 1), :], sem.at[slot])

    def issue(tile, slot):
        base = tile * tm

        def body(g, carry):
            r0 = pl.multiple_of(g * SUBLANES, SUBLANES)
            for j in range(SUBLANES):
                row_copy(slot, 0, r0 + j, d0_ref[base + r0 + j]).start(priority=0)
                row_copy(slot, 1, r0 + j, d1_ref[base + r0 + j]).start(priority=1)
            return carry

        lax.fori_loop(0, tm // SUBLANES, body, 0)

    @pl.when(i == 0)
    def _():
        issue(0, 0)

    @pl.when(i + 1 < n_tiles)
    def _():
        issue(i + 1, lax.rem(i + 1, 2))

    slot = lax.rem(i, 2)

    def drain(g, carry):
        for _ in range(SUBLANES):
            row_copy(slot, 0, 0, 0).wait()
            row_copy(slot, 1, 0, 0).wait()
        return carry

    lax.fori_loop(0, tm // SUBLANES, drain, 0)

    w12 = w12_ref[...]
    reps = x1_ref.shape[1] // LANES
    w1 = jnp.concatenate([w12[:, :LANES]] * reps, axis=1)
    w2 = jnp.concatenate([w12[:, LANES:]] * reps, axis=1)
    x2 = x1_ref[...] + (w1 * buf[slot, 0] + w2 * buf[slot, 1])
    ms = jnp.mean(x2 * x2, axis=-1, keepdims=True)
    xn = x2 * lax.rsqrt(ms + RMS_EPS) * nw_ref[...]
    if final:
        split, split_val = outs, xn
    else:
        split, split_val = outs[:2], x2
        outs[2][...] = xn.astype(outs[2].dtype)

    @pl.when(i < n_prompt_tiles)
    def _():
        split[0][...] = split_val

    @pl.when(i >= n_prompt_tiles)
    def _():
        split[1][...] = split_val


def _combine(x1, w12, yg, dest0, dest1, norm_w, n_p, tm, final):
    n, d = x1.shape
    n_tiles = n // tm
    npt = n_p // tm
    spec_p = pl.BlockSpec((tm, d), lambda i, *_: (jnp.minimum(i, npt - 1), 0))
    spec_s = pl.BlockSpec((tm, d), lambda i, *_: (jnp.maximum(i - npt, 0), 0))
    out_specs = [spec_p, spec_s]
    out_shape = [jax.ShapeDtypeStruct((n_p, d), F32), jax.ShapeDtypeStruct((n - n_p, d), F32)]
    if not final:
        out_specs.append(pl.BlockSpec((tm, d), lambda i, *_: (i, 0)))
        out_shape.append(jax.ShapeDtypeStruct((n, d), BF16))
    grid_spec = pltpu.PrefetchScalarGridSpec(
        num_scalar_prefetch=2,
        grid=(n_tiles,),
        in_specs=[
            pl.BlockSpec((tm, d), lambda i, *_: (i, 0)),
            pl.BlockSpec((tm, 2 * LANES), lambda i, *_: (i, 0)),
            pl.BlockSpec((1, d), lambda i, *_: (0, 0)),
            pl.BlockSpec(memory_space=pl.ANY),
        ],
        out_specs=out_specs,
        scratch_shapes=[pltpu.VMEM((2, 2, tm, d), F32), pltpu.SemaphoreType.DMA((2,))],
    )
    return pl.pallas_call(
        functools.partial(_combine_kernel, tm=tm, n_tiles=n_tiles, n_prompt_tiles=npt, final=final),
        grid_spec=grid_spec,
        out_shape=out_shape,
        compiler_params=_cparams("arbitrary"),
        name="moe_combine",
    )(dest0, dest1, x1, w12, norm_w.reshape(1, d), yg)


def _pad_lanes(v, width=LANES):
    v = v.reshape(1, -1)
    return jnp.pad(v, ((0, 0), (0, width - v.shape[1])))


def _carry_rows(state, width):
    return jnp.pad(state, ((0, 0), (SUBLANES - (width - 1), 0), (0, 0)))


def _moe(x1, ffn_norm_w, logits, layer, w_gate, w_up, w_down, next_norm_w, n_p, final):
    n = x1.shape[0]
    info, w12, cnt = _route(logits, ROUTE_TM)
    counts = cnt[0, :N_EXPERTS].astype(jnp.int32)
    er = info[:, 0:4].T.astype(jnp.int32)
    plan = _visit_plan(counts, (2 * n) // FFN_TM, FFN_TM)
    ids = jnp.arange(N_EXPERTS, dtype=jnp.int32)[:, None]

    def first_row(e):
        return jnp.sum(jnp.where(ids < e[None, :], counts[:, None], 0), axis=0)

    dest0 = first_row(er[0]) + er[2]
    dest1 = first_row(er[1]) + er[3]
    xg = _dispatch(x1, ffn_norm_w, dest0, dest1, DISPATCH_TM)
    yg = _ffn(xg, w_gate, w_up, w_down, layer, plan, FFN_TM)
    return _combine(x1, w12, yg, dest0, dest1, next_norm_w, n_p, COMBINE_TM, final)


def kernel(x_prompt, x_sample, state_ssm_conv, state_ssm, state_lru_conv, state_lru, state_sconv, norm_mix, norm_ffn, norm_final, w_ssm_in, ssm_conv_w, ssm_conv_b, ssm_dt_bias, ssm_A_log, ssm_D, ssm_norm_w, w_ssm_out, w_lru_in, lru_conv_w, lru_conv_b, lru_wa, lru_ba, lru_wi, lru_bi, lru_lambda, w_lru_out, w_sc_in, sc_conv_w, w_sc_out, w_route_group, w_route_expert, w_exp_gate, w_exp_up, w_exp_down):
    b_p, t_p, d = x_prompt.shape
    b_s, t_s, _ = x_sample.shape
    n_p = b_p * t_p
    n_s = b_s * t_s
    n = n_p + n_s
    assert n_s == ROW_TM and n_p % ROW_TM == 0 and t_p % SSM_CHUNK == 0 and t_s <= SAMPLE_PAD_T
    mm_tm = n // 4
    n_chunks_p = t_p // SSM_CHUNK
    n_a = state_ssm.shape[0]

    x_p = x_prompt.reshape(n_p, d)
    x_s = x_sample.reshape(n_s, d)
    xn = _rmsnorm(x_p, x_s, norm_mix[0], ROW_TM, BF16)

    def last_rows_p(m, col0, ncol, k):
        return jnp.stack([m[(b + 1) * t_p - k:(b + 1) * t_p, col0:col0 + ncol] for b in range(b_p)])

    def last_rows_s(state, m, col0, ncol):
        new = m[n_p:, col0:col0 + ncol].reshape(b_s, t_s, ncol)
        return jnp.concatenate([state, new], axis=1)[:, t_s:]

    ssd_h_p = ssd_h_s = None
    ssd_h0_p = jnp.zeros((n_a * b_p, SSM_INNER, SSM_STATE), F32)
    ssd_h0_s = state_ssm.reshape(n_a * b_s, SSM_INNER, SSM_STATE)
    w_ssm_in_t = jnp.swapaxes(w_ssm_in, 1, 2)

    p_ssm_conv, p_lru_conv, p_lru, p_sconv = [], [], [], []
    s_ssm_conv, s_lru_conv, s_lru, s_sconv = [], [], [], []

    for i in range(DEPTH):
        kind = i % N_MIXERS
        s = i // N_MIXERS
        w_r = jnp.pad(jnp.concatenate([w_route_group[i], w_route_expert[i]], axis=1),
                      ((0, 0), (0, LANES - MOE_GROUPS - N_EXPERTS)))
        r_hi = w_r.astype(BF16)
        r_lo = (w_r - r_hi.astype(F32)).astype(BF16)
        if kind == 0:
            z = _matmul(xn, w_ssm_in_t, s, 0, SSM_INNER, PROJ_TN, mm_tm, transposed=True)
            xbc = _matmul(xn, w_ssm_in_t, s, SSM_INNER // PROJ_TN, SSM_CONV_DIM, PROJ_TN, mm_tm, transposed=True)
            w_dt = jnp.pad(w_ssm_in_t[s, SSM_INNER + SSM_CONV_DIM:, :], ((0, LANES - SSM_HEADS), (0, 0)))[None]
            dt_raw = _matmul(xn, w_dt, 0, 0, LANES, LANES, mm_tm, transposed=True)
            params = (ssm_conv_w[s], ssm_conv_b[s].reshape(1, -1), _pad_lanes(ssm_dt_bias[s]),
                      _pad_lanes(ssm_A_log[s]), jnp.repeat(ssm_D[s], SSM_HEAD_DIM).reshape(1, -1),
                      ssm_norm_w[s].reshape(1, -1))
            act_p, ssd_h_p = _ssd_core(
                z, xbc, dt_raw, jnp.zeros((b_p, SUBLANES, SSM_CONV_DIM), F32), ssd_h0_p, *params, ssd_h_p,
                nseq=b_p, rows=SSM_CHUNK, n_chunks=n_chunks_p, t_valid=SSM_CHUNK, seq0=s * b_p, n_slots=n_a * b_p)
            act_s, ssd_h_s = _ssd_core(
                z, xbc, dt_raw, _carry_rows(state_ssm_conv[s], SSM_CONV), ssd_h0_s, *params, ssd_h_s,
                nseq=b_s, rows=SAMPLE_PAD_T, n_chunks=1, t_valid=t_s, seq0=s * b_s, n_slots=n_a * b_s,
                packed_row0=n_p)
            p_ssm_conv.append(last_rows_p(xbc, 0, SSM_CONV_DIM, SSM_CONV - 1))
            s_ssm_conv.append(last_rows_s(state_ssm_conv[s], xbc, 0, SSM_CONV_DIM))
            w_out = w_ssm_out
        elif kind == 1:
            proj = _matmul(xn, w_lru_in, s, 0, 2 * LRU_WIDTH, PROJ_TN, mm_tm)
            params = (lru_conv_w[s], lru_conv_b[s].reshape(1, -1), lru_wa[s], lru_ba[s].reshape(1, -1), lru_wi[s],
                      lru_bi[s].reshape(1, -1), lru_lambda[s].reshape(1, -1))
            act_p, h_p = _lru_core(
                proj, jnp.zeros((b_p, SUBLANES, LRU_WIDTH), F32), jnp.zeros((b_p, 1, LRU_WIDTH), F32), *params,
                nseq=b_p, rows=SSM_CHUNK, n_chunks=n_chunks_p, t_valid=SSM_CHUNK)
            act_s, h_s = _lru_core(
                proj, _carry_rows(state_lru_conv[s], LRU_CONV), state_lru[s].reshape(b_s, 1, LRU_WIDTH), *params,
                nseq=b_s, rows=SAMPLE_PAD_T, n_chunks=1, t_valid=t_s, packed_row0=n_p)
            p_lru_conv.append(last_rows_p(proj, LRU_WIDTH, LRU_WIDTH, LRU_CONV - 1))
            s_lru_conv.append(last_rows_s(state_lru_conv[s], proj, LRU_WIDTH, LRU_WIDTH))
            p_lru.append(h_p.reshape(b_p, LRU_WIDTH))
            s_lru.append(h_s.reshape(b_s, LRU_WIDTH))
            w_out = w_lru_out
        else:
            proj = _matmul(xn, w_sc_in, s, 0, 3 * d, PROJ_TN, mm_tm)
            act_p, v_p = _sc_core(proj, jnp.zeros((b_p, SUBLANES, d), F32), sc_conv_w[s],
                                  nseq=b_p, rows=SSM_CHUNK, n_chunks=n_chunks_p, t_valid=SSM_CHUNK)
            act_s, v_s = _sc_core(proj, _carry_rows(state_sconv[s], SC_WIDTH), sc_conv_w[s],
                                  nseq=b_s, rows=SAMPLE_PAD_T, n_chunks=1, t_valid=t_s, packed_row0=n_p)
            p_sconv.append(v_p[:, SUBLANES - (SC_WIDTH - 1):])
            s_sconv.append(jnp.concatenate([state_sconv[s], v_s[:, :t_s]], axis=1)[:, t_s:])
            w_out = w_sc_out

        x1, logits = _outproj(act_p, act_s, w_out[s].astype(BF16), x_p, x_s, norm_ffn[i], r_hi, r_lo, PROJ_TK)
        moe_w = (w_exp_gate, w_exp_up, w_exp_down)
        if i < DEPTH - 1:
            x_p, x_s, xn = _moe(x1, norm_ffn[i], logits, i, *moe_w, norm_mix[i + 1], n_p, False)
        else:
            y_p, y_s = _moe(x1, norm_ffn[i], logits, i, *moe_w, norm_final, n_p, True)

    state_shape = (SSM_HEADS, SSM_HEAD_DIM, SSM_STATE)
    return (y_p.reshape(b_p, t_p, d), y_s.reshape(b_s, t_s, d),
            jnp.stack(p_ssm_conv), ssd_h_p.reshape((n_a, b_p) + state_shape), jnp.stack(p_lru_conv), jnp.stack(p_lru),
            jnp.stack(p_sconv),
            jnp.stack(s_ssm_conv), ssd_h_s.reshape((n_a, b_s) + state_shape), jnp.stack(s_lru_conv), jnp.stack(s_lru),
            jnp.stack(s_sconv))
```

```python
import functools
import math

import jax
import jax.numpy as jnp
from jax import lax
from jax.experimental import pallas as pl
from jax.experimental.pallas import tpu as pltpu

F32 = jnp.float32
BF16 = jnp.bfloat16

D_MODEL = 2048
DEPTH = 4
N_MIXERS = 3
RMS_EPS = 1e-6
LOG2_E = 1.4426950408889634
SSM_INNER = 4096
SSM_HEAD_DIM = 64
SSM_HEADS = 64
SSM_GROUPS = 8
SSM_HPG = 8
SSM_STATE = 128
SSM_CONV = 4
SSM_BC = SSM_GROUPS * SSM_STATE
SSM_CONV_DIM = SSM_INNER + 2 * SSM_BC
SSM_CHUNK = 128
LRU_WIDTH = 2048
LRU_HEADS = 16
LRU_BLOCK = 128
LRU_CONV = 4
LRU_C = 8.0
SC_WIDTH = 3
MOE_GROUPS = 8
MOE_PER_GROUP = 4
N_EXPERTS = 32
MOE_FF = 512

LANES = 128
SUBLANES = 8
SAMPLE_PAD_T = 8
VMEM_LIMIT = 56 * 1024 * 1024

ROW_TM = 512
PROJ_TN = 512
PROJ_TK = 1024
ROUTE_TM = 512
DISPATCH_TM = 512
FFN_TM = 512
COMBINE_TM = 512


def _cparams(*sem):
    return pltpu.CompilerParams(dimension_semantics=sem, vmem_limit_bytes=VMEM_LIMIT)


def _silu(x):
    return x * jax.nn.sigmoid(x)


def _softplus(x):
    return jnp.maximum(x, 0.0) + jnp.log1p(jnp.exp(-jnp.abs(x)))


def _dot(a, b):
    return jnp.dot(a, b, preferred_element_type=F32)


def _dot_nt(a, b):
    return lax.dot_general(a, b, (((1,), (1,)), ((), ())), preferred_element_type=F32)


def _split3(x):
    hi = x.astype(BF16)
    r = x - hi.astype(F32)
    mid = r.astype(BF16)
    lo = (r - mid.astype(F32)).astype(BF16)
    return hi, mid, lo


def _norm_kernel(xp_ref, xs_ref, w_ref, o_ref, *, n_prompt_tiles):
    def emit(x):
        ms = jnp.mean(x * x, axis=-1, keepdims=True)
        o_ref[...] = (x * lax.rsqrt(ms + RMS_EPS) * w_ref[...]).astype(o_ref.dtype)

    @pl.when(pl.program_id(0) < n_prompt_tiles)
    def _():
        emit(xp_ref[...])

    @pl.when(pl.program_id(0) >= n_prompt_tiles)
    def _():
        emit(xs_ref[...])


def _rmsnorm(x_p, x_s, w, tm, out_dtype):
    n_p, d = x_p.shape
    n = n_p + x_s.shape[0]
    npt = n_p // tm
    return pl.pallas_call(
        functools.partial(_norm_kernel, n_prompt_tiles=npt),
        grid=(n // tm,),
        in_specs=[
            pl.BlockSpec((tm, d), lambda i: (jnp.minimum(i, npt - 1), 0)),
            pl.BlockSpec((tm, d), lambda i: (jnp.maximum(i - npt, 0), 0)),
            pl.BlockSpec((1, d), lambda i: (0, 0)),
        ],
        out_specs=pl.BlockSpec((tm, d), lambda i: (i, 0)),
        out_shape=jax.ShapeDtypeStruct((n, d), out_dtype),
        compiler_params=_cparams("arbitrary"),
        name="rmsnorm",
    )(x_p, x_s, w.reshape(1, d))


def _mm_kernel(a_ref, w_ref, o_ref, *, transposed):
    w = w_ref[...].astype(BF16)
    o_ref[...] = _dot_nt(a_ref[...], w) if transposed else _dot(a_ref[...], w)


def _matmul(a, w_stack, layer, col_blk0, ncols, tn, tm, transposed=False):
    n, k = a.shape
    if transposed:
        w_spec = pl.BlockSpec((None, tn, k), lambda i, j: (layer, col_blk0 + j, 0))
    else:
        w_spec = pl.BlockSpec((None, k, tn), lambda i, j: (layer, 0, col_blk0 + j))
    return pl.pallas_call(
        functools.partial(_mm_kernel, transposed=transposed),
        grid=(n // tm, ncols // tn),
        in_specs=[pl.BlockSpec((tm, k), lambda i, j: (i, 0)), w_spec],
        out_specs=pl.BlockSpec((tm, tn), lambda i, j: (i, j)),
        out_shape=jax.ShapeDtypeStruct((n, ncols), F32),
        compiler_params=_cparams("parallel", "arbitrary"),
        name="in_proj",
    )(a, w_stack)


def _outproj_kernel(ap_ref, as_ref, w_ref, xp_ref, xs_ref, nw_ref, rh_ref, rl_ref, x1_ref, lg_ref, acc_ref,
                    *, n_prompt_tiles, nk, tk):
    i = pl.program_id(0)
    k = pl.program_id(1)
    w = w_ref[pl.ds(pl.multiple_of(k * tk, tk), tk), :]

    @pl.when(i < n_prompt_tiles)
    def _():
        @pl.when(k == 0)
        def _():
            acc_ref[...] = xp_ref[...]

        acc_ref[...] += _dot(ap_ref[...], w)

    @pl.when(i >= n_prompt_tiles)
    def _():
        @pl.when(k == 0)
        def _():
            acc_ref[...] = xs_ref[...]

        acc_ref[...] += _dot(as_ref[...].astype(BF16), w)

    @pl.when(k == nk - 1)
    def _():
        x1 = acc_ref[...]
        x1_ref[...] = x1
        ms = jnp.mean(x1 * x1, axis=-1, keepdims=True)
        xn = x1 * lax.rsqrt(ms + RMS_EPS) * nw_ref[...]
        hi = xn.astype(BF16)
        lo = (xn - hi.astype(F32)).astype(BF16)
        rh = rh_ref[...]
        lg_ref[...] = _dot(hi, rh) + (_dot(hi, rl_ref[...]) + _dot(lo, rh))


def _outproj(act_p, act_s, w_bf16, x_p, x_s, norm_w, r_hi, r_lo, tk):
    n_p, kdim = act_p.shape
    d = x_p.shape[1]
    tm = act_s.shape[0]
    n = n_p + tm
    npt = n_p // tm
    nk = kdim // tk
    kern = functools.partial(_outproj_kernel, n_prompt_tiles=npt, nk=nk, tk=tk)
    return pl.pallas_call(
        kern,
        grid=(n // tm, nk),
        in_specs=[
            pl.BlockSpec((tm, tk), lambda i, k: (jnp.minimum(i, npt - 1), jnp.where(i < npt, k, nk - 1))),
            pl.BlockSpec((tm, tk), lambda i, k: (0, jnp.where(i >= npt, k, 0))),
            pl.BlockSpec((kdim, d), lambda i, k: (0, 0), pipeline_mode=pl.Buffered(1)),
            pl.BlockSpec((tm, d), lambda i, k: (jnp.minimum(i, npt - 1), 0)),
            pl.BlockSpec((tm, d), lambda i, k: (0, 0), pipeline_mode=pl.Buffered(1)),
            pl.BlockSpec((1, d), lambda i, k: (0, 0)),
            pl.BlockSpec((d, LANES), lambda i, k: (0, 0)),
            pl.BlockSpec((d, LANES), lambda i, k: (0, 0)),
        ],
        out_specs=[
            pl.BlockSpec((tm, d), lambda i, k: (i, 0)),
            pl.BlockSpec((tm, LANES), lambda i, k: (i, 0)),
        ],
        out_shape=[
            jax.ShapeDtypeStruct((n, d), F32),
            jax.ShapeDtypeStruct((n, LANES), F32),
        ],
        scratch_shapes=[pltpu.VMEM((tm, d), F32)],
        compiler_params=_cparams("arbitrary", "arbitrary"),
        name="out_proj",
    )(act_p, act_s, w_bf16, x_p, x_s, norm_w.reshape(1, d), r_hi, r_lo)


def _causal_conv(carry_ref, x, w_ref, bias, carry0_ref, width, rows, first):
    @pl.when(first)
    def _():
        carry_ref[...] = carry0_ref[...]

    n_tiles = rows // SUBLANES
    tiles = [carry_ref[...]] + [x[i * SUBLANES:(i + 1) * SUBLANES, :] for i in range(n_tiles)]
    sub = lax.broadcasted_iota(jnp.int32, tiles[0].shape, 0)
    y = None
    for k in range(width):
        back = width - 1 - k
        if back == 0:
            tap = x
        else:
            rolled = [pltpu.roll(t, back, 0) for t in tiles]
            tap = jnp.concatenate(
                [jnp.where(sub < back, rolled[i], rolled[i + 1]) for i in range(n_tiles)], axis=0)
        term = tap * w_ref[k:k + 1, :]
        y = term if y is None else y + term
    if bias is not None:
        y = y + bias
    carry_ref[...] = tiles[-1]
    return y


def _pad_rows(a, rows):
    if a.shape[0] == rows:
        return a
    return jnp.concatenate([a, jnp.zeros((rows - a.shape[0], a.shape[1]), a.dtype)], axis=0)


def _loader(packed, t_valid, second=None):
    if not packed:
        return lambda ref: ref[...]
    if second is None:
        second = lax.rem(pl.program_id(0), 2) == 1

    def load(ref):
        x8 = ref[...]
        if isinstance(second, bool):
            x8 = pltpu.roll(x8, SUBLANES - t_valid, 0) if second else x8
        else:
            x8 = jnp.where(second, pltpu.roll(x8, SUBLANES - t_valid, 0), x8)
        sub = lax.broadcasted_iota(jnp.int32, x8.shape, 0)
        x8 = jnp.where(sub < t_valid, x8, 0.0)
        return _pad_rows(x8, SAMPLE_PAD_T)

    return load


def _row_blocks(nseq, rows, n_chunks, t_valid, packed_row0):
    if packed_row0 is None:
        last = nseq * n_chunks - 1
        rmap = lambda b, c: (jnp.where(b < nseq, b * n_chunks + c, last), 0)
        return rows, rmap, rmap, nseq * n_chunks * rows, BF16
    assert n_chunks == 1 and 2 * t_valid == SUBLANES and nseq % 2 == 0 and packed_row0 % SUBLANES == 0
    blk0 = packed_row0 // SUBLANES
    imap = lambda b, c: (blk0 + jnp.minimum(b, nseq - 1) // 2, 0)
    omap = lambda b, c: (jnp.minimum(b, nseq - 1) // 2, 0)
    return SUBLANES, imap, omap, nseq * t_valid, F32


def _store_act(act_ref, act, packed, t_valid):
    if not packed:
        act_ref[...] = act.astype(act_ref.dtype)
        return
    second = lax.rem(pl.program_id(0), 2) == 1
    a8 = act[0:SUBLANES, :].astype(act_ref.dtype)

    @pl.when(jnp.logical_not(second))
    def _():
        act_ref[...] = a8

    @pl.when(second)
    def _():
        sub = lax.broadcasted_iota(jnp.int32, a8.shape, 0)
        act_ref[...] = jnp.where(sub < t_valid, act_ref[...], pltpu.roll(a8, t_valid, 0))


def _ssd_kernel(*refs, n_steps, n_fill, aliased, **static):
    if aliased:
        refs = refs[:11] + refs[12:]
    data, params, act_ref, (ht_ref, xp_ref, h_ref) = refs[:5], refs[5:11], refs[11], refs[12:]
    t_valid = static["t_valid"]

    def run():
        if not static["packed"]:
            act = _ssd_body(*data, *params, ht_ref, xp_ref, h_ref, second=None, **static)
            act_ref[...] = act.astype(act_ref.dtype)
            return
        z_ref, xbc_ref, dt_ref, carry0_ref, h0_ref = data
        acts = [_ssd_body(z_ref, xbc_ref, dt_ref, carry0_ref.at[q], h0_ref.at[q], *params, ht_ref.at[q],
                          xp_ref.at[q], h_ref, second=bool(q), **static)[0:SUBLANES, :] for q in (0, 1)]
        sub = lax.broadcasted_iota(jnp.int32, acts[0].shape, 0)
        act_ref[...] = jnp.where(sub < t_valid, acts[0], pltpu.roll(acts[1], t_valid, 0)).astype(act_ref.dtype)

    if n_fill == 0:
        run()
        return

    @pl.when(pl.program_id(0) < n_steps)
    def _():
        run()

    @pl.when(pl.program_id(0) >= n_steps)
    def _():
        ht_ref[...] = jnp.zeros_like(ht_ref)


def _ssd_body(z_ref, xbc_ref, dt_ref, carry0_ref, h0_ref, cw_ref, cb_ref, dtb_ref, alog_ref, dexp_ref, nw_ref,
              ht_ref, xp_ref, h_ref, *, rows, n_chunks, t_valid, packed, second):
    c = pl.program_id(1)
    lp = SSM_CHUNK
    load = _loader(packed, t_valid, second)
    if n_chunks == 1:
        h_in, h_out = h0_ref, ht_ref
    else:
        h_in = h_out = h_ref

        @pl.when(c == 0)
        def _():
            h_ref[...] = h0_ref[...]

    conv = _causal_conv(xp_ref, load(xbc_ref), cw_ref, cb_ref[...], carry0_ref, SSM_CONV, rows, c == 0)
    xbc = _silu(conv)
    xs = xbc[:, :SSM_INNER]
    bm = xbc[:, SSM_INNER:SSM_INNER + SSM_BC]
    cm = xbc[:, SSM_INNER + SSM_BC:]

    row_q = lax.broadcasted_iota(jnp.int32, (rows, LANES), 0)
    dt = _softplus(load(dt_ref) + dtb_ref[...])
    if t_valid < rows:
        dt = jnp.where(row_q < t_valid, dt, 0.0)
    a_neg = -jnp.exp(alog_ref[...])
    dt_p = _pad_rows(dt, lp)
    ad = dt_p * a_neg
    r_i = lax.broadcasted_iota(jnp.int32, (lp, lp), 0)
    c_i = lax.broadcasted_iota(jnp.int32, (lp, lp), 1)
    tri = jnp.where(c_i <= r_i, 1.0, 0.0).astype(BF16)
    a_hi, a_mid, a_lo = _split3(ad)
    cs = _dot(tri, a_hi) + (_dot(tri, a_mid) + _dot(tri, a_lo))
    cs2 = cs * LOG2_E
    cs_t = cs2.T
    dt_t = dt_p.T
    cs_last = cs[lp - 1:lp, :]
    to_end = jnp.exp(cs_last - cs)
    w_t = (dt_p * to_end).T
    e_last_t = jnp.broadcast_to(jnp.exp(cs_last), (lp, LANES)).T
    cs_q = cs2[:rows, :]
    lane_q = lax.broadcasted_iota(jnp.int32, (rows, LANES), 1)
    causal = lax.broadcasted_iota(jnp.int32, (rows, lp), 1) <= lax.broadcasted_iota(jnp.int32, (rows, lp), 0)
    lane_lo = lane_q < SSM_HEAD_DIM

    y_groups = []
    for g in range(SSM_GROUPS):
        cg = cm[:, g * SSM_STATE:(g + 1) * SSM_STATE].astype(BF16)
        bg = _pad_rows(bm[:, g * SSM_STATE:(g + 1) * SSM_STATE], lp).astype(BF16)
        gw = SSM_HPG * SSM_HEAD_DIM
        hg = h_in[g * gw:(g + 1) * gw, :]
        cb = _dot_nt(cg, bg)
        yoff = _dot_nt(cg, hg.astype(BF16))
        xs_g = _pad_rows(xs[:, g * gw:(g + 1) * gw], lp)
        y_pairs = []
        for j in range(SSM_HPG // 2):
            ha = g * SSM_HPG + 2 * j
            hb = ha + 1
            col_a = jnp.sum(jnp.where(lane_q == ha, cs_q, 0.0), axis=1, keepdims=True)
            col_b = jnp.sum(jnp.where(lane_q == hb, cs_q, 0.0), axis=1, keepdims=True)
            m_a = cb * jnp.exp2(jnp.where(causal, col_a - cs_t[ha:ha + 1, :], -jnp.inf)) * dt_t[ha:ha + 1, :]
            m_b = cb * jnp.exp2(jnp.where(causal, col_b - cs_t[hb:hb + 1, :], -jnp.inf)) * dt_t[hb:hb + 1, :]
            xs_pair = xs_g[:, j * LANES:(j + 1) * LANES].astype(BF16)
            ydiag = jnp.where(lane_lo, _dot(m_a.astype(BF16), xs_pair), _dot(m_b.astype(BF16), xs_pair))
            scale = jnp.where(lane_lo, jnp.exp2(col_a), jnp.exp2(col_b))
            y_pairs.append(ydiag + yoff[:, j * LANES:(j + 1) * LANES] * scale)
        y_groups.append(jnp.concatenate(y_pairs, axis=1))
        xs_t = xs_g.T
        w_rows = jnp.concatenate(
            [jnp.broadcast_to(w_t[g * SSM_HPG + j:g * SSM_HPG + j + 1, :], (SSM_HEAD_DIM, lp)) for j in range(SSM_HPG)],
            axis=0)
        d_rows = jnp.concatenate(
            [jnp.broadcast_to(e_last_t[g * SSM_HPG + j:g * SSM_HPG + j + 1, :], (SSM_HEAD_DIM, LANES))
             for j in range(SSM_HPG)], axis=0)
        st = _dot((xs_t * w_rows).astype(BF16), bg)
        h_out[g * gw:(g + 1) * gw, :] = hg * d_rows + st

    y = jnp.concatenate(y_groups, axis=1)
    y = y + xs * dexp_ref[...]
    y = y * _silu(load(z_ref))
    gsz = SSM_INNER // SSM_GROUPS
    outs = []
    for g in range(SSM_GROUPS):
        yg = y[:, g * gsz:(g + 1) * gsz]
        ms = jnp.mean(yg * yg, axis=-1, keepdims=True)
        outs.append(yg * lax.rsqrt(ms + RMS_EPS))
    if n_chunks > 1:
        @pl.when(c == n_chunks - 1)
        def _():
            ht_ref[...] = h_ref[...]

    return jnp.concatenate(outs, axis=1) * nw_ref[...]


def _ssd_core(z, xbc, dt_raw, carry0, h0, conv_w, conv_b, dt_bias, a_log, d_exp, norm_w, ht_buf, *, nseq, rows,
              n_chunks, t_valid, seq0, n_slots, packed_row0=None):
    aliased = ht_buf is not None
    packed = packed_row0 is not None
    per_step = 2 if packed else 1
    n_fill_slots = 0 if aliased else n_slots - seq0 - nseq
    assert nseq % per_step == 0 and seq0 % per_step == 0 and n_fill_slots % per_step == 0
    n_steps, n_fill, step0 = nseq // per_step, n_fill_slots // per_step, seq0 // per_step
    kern = functools.partial(_ssd_kernel, n_steps=n_steps, n_fill=n_fill, aliased=aliased, rows=rows,
                             n_chunks=n_chunks, t_valid=t_valid, packed=packed)
    seq_dim = 2 if packed else None
    if packed:
        assert n_chunks == 1 and 2 * t_valid == SUBLANES and packed_row0 % SUBLANES == 0
        blk_rows, act_rows, act_dtype = SUBLANES, nseq * t_valid, F32
        rmap = lambda b, c: (packed_row0 // SUBLANES + jnp.minimum(b, n_steps - 1), 0)
        omap = lambda b, c: (jnp.minimum(b, n_steps - 1), 0)
    else:
        blk_rows, rmap, omap, act_rows, act_dtype = _row_blocks(nseq, rows, n_chunks, t_valid, None)
    smap = lambda b, c: (jnp.minimum(b, n_steps - 1), 0, 0)
    const = lambda b, c: (0, 0)
    in_specs = [
        pl.BlockSpec((blk_rows, SSM_INNER), rmap),
        pl.BlockSpec((blk_rows, SSM_CONV_DIM), rmap),
        pl.BlockSpec((blk_rows, LANES), rmap),
        pl.BlockSpec((seq_dim, SUBLANES, SSM_CONV_DIM), smap),
        pl.BlockSpec((seq_dim, SSM_INNER, SSM_STATE), lambda b, c: (step0 + jnp.minimum(b, n_steps - 1), 0, 0)),
        pl.BlockSpec((SSM_CONV, SSM_CONV_DIM), const),
        pl.BlockSpec((1, SSM_CONV_DIM), const),
        pl.BlockSpec((1, LANES), const),
        pl.BlockSpec((1, LANES), const),
        pl.BlockSpec((1, SSM_INNER), const),
        pl.BlockSpec((1, SSM_INNER), const),
    ]
    args = [z, xbc, dt_raw, carry0, h0, conv_w, conv_b, dt_bias, a_log, d_exp, norm_w]
    if aliased:
        in_specs.append(pl.BlockSpec(memory_space=pl.ANY))
        args.append(ht_buf)
    return pl.pallas_call(
        kern,
        grid=(n_steps + n_fill, n_chunks),
        in_specs=in_specs,
        out_specs=[
            pl.BlockSpec((blk_rows, SSM_INNER), omap),
            pl.BlockSpec((seq_dim, SSM_INNER, SSM_STATE), lambda b, c: (step0 + b, 0, 0)),
        ],
        out_shape=[
            jax.ShapeDtypeStruct((act_rows, SSM_INNER), act_dtype),
            jax.ShapeDtypeStruct((n_slots, SSM_INNER, SSM_STATE), F32),
        ],
        scratch_shapes=[
            pltpu.VMEM((2, SUBLANES, SSM_CONV_DIM) if packed else (SUBLANES, SSM_CONV_DIM), F32),
            pltpu.VMEM((SSM_INNER, SSM_STATE), F32),
        ],
        input_output_aliases={11: 1} if aliased else {},
        compiler_params=_cparams("arbitrary", "arbitrary"),
        name="ssd_core",
    )(*args)


def _lru_kernel(gin_ref, xr_ref, carry0_ref, h0_ref, cw_ref, cb_ref, wa_ref, ba_ref, wi_ref, bi_ref, lam_ref,
                act_ref, ht_ref, xp_ref, hc_ref, *, rows, n_chunks, t_valid, packed):
    c = pl.program_id(1)
    load = _loader(packed, t_valid)

    @pl.when(c == 0)
    def _():
        hc_ref[...] = h0_ref[...]

    xc = _causal_conv(xp_ref, load(xr_ref), cw_ref, cb_ref[...], carry0_ref, LRU_CONV, rows, c == 0)
    ra = []
    ia = []
    for h in range(LRU_HEADS):
        xh = xc[:, h * LRU_BLOCK:(h + 1) * LRU_BLOCK].astype(BF16)
        ra.append(_dot(xh, wa_ref[h].astype(BF16)))
        ia.append(_dot(xh, wi_ref[h].astype(BF16)))
    r = jax.nn.sigmoid(jnp.concatenate(ra, axis=1) + ba_ref[...])
    ig = jax.nn.sigmoid(jnp.concatenate(ia, axis=1) + bi_ref[...])
    log_a = -LRU_C * r * _softplus(-lam_ref[...])
    a = jnp.exp(log_a)
    th = jnp.tanh(log_a)
    mult = jnp.sqrt(-2.0 * th / (1.0 - th))
    u = mult * (ig * xc)
    if t_valid < rows:
        row = lax.broadcasted_iota(jnp.int32, (rows, LRU_WIDTH), 0)
        a = jnp.where(row < t_valid, a, 1.0)
        u = jnp.where(row < t_valid, u, 0.0)

    sub = lax.broadcasted_iota(jnp.int32, (SUBLANES, LRU_WIDTH), 0)
    h_prev = hc_ref[...]
    h_blocks = []
    for blk in range(rows // SUBLANES):
        ab = a[blk * SUBLANES:(blk + 1) * SUBLANES, :]
        ub = u[blk * SUBLANES:(blk + 1) * SUBLANES, :]
        for d in (1, 2, 4):
            a_sh = jnp.where(sub >= d, pltpu.roll(ab, d, 0), 1.0)
            u_sh = jnp.where(sub >= d, pltpu.roll(ub, d, 0), 0.0)
            ub = ub + ab * u_sh
            ab = ab * a_sh
        hb = ub + ab * h_prev
        h_blocks.append(hb)
        h_prev = hb[SUBLANES - 1:SUBLANES, :]
    hc_ref[...] = h_prev
    hseq = jnp.concatenate(h_blocks, axis=0)
    gx = load(gin_ref)
    gate = 0.5 * gx * (1.0 + jnp.tanh(math.sqrt(2.0 / math.pi) * (gx + 0.044715 * (gx * gx * gx))))
    _store_act(act_ref, hseq * gate, packed, t_valid)

    @pl.when(c == n_chunks - 1)
    def _():
        ht_ref[...] = hc_ref[...]


def _lru_core(proj, carry0, h0, conv_w, conv_b, wa, ba, wi, bi, lam, *, nseq, rows, n_chunks, t_valid,
              packed_row0=None):
    kern = functools.partial(_lru_kernel, rows=rows, n_chunks=n_chunks, t_valid=t_valid,
                             packed=packed_row0 is not None)
    blk_rows, rmap, omap, act_rows, act_dtype = _row_blocks(nseq, rows, n_chunks, t_valid, packed_row0)
    w = LRU_WIDTH
    const = lambda b, c: (0, 0)
    const3 = lambda b, c: (0, 0, 0)
    return pl.pallas_call(
        kern,
        grid=(nseq, n_chunks),
        in_specs=[
            pl.BlockSpec((blk_rows, w), rmap),
            pl.BlockSpec((blk_rows, w), lambda b, c: (rmap(b, c)[0], 1)),
            pl.BlockSpec((None, SUBLANES, w), lambda b, c: (b, 0, 0)),
            pl.BlockSpec((None, 1, w), lambda b, c: (b, 0, 0)),
            pl.BlockSpec((LRU_CONV, w), const),
            pl.BlockSpec((1, w), const),
            pl.BlockSpec((LRU_HEADS, LRU_BLOCK, LRU_BLOCK), const3),
            pl.BlockSpec((1, w), const),
            pl.BlockSpec((LRU_HEADS, LRU_BLOCK, LRU_BLOCK), const3),
            pl.BlockSpec((1, w), const),
            pl.BlockSpec((1, w), const),
        ],
        out_specs=[
            pl.BlockSpec((blk_rows, w), omap),
            pl.BlockSpec((None, 1, w), lambda b, c: (b, 0, 0)),
        ],
        out_shape=[
            jax.ShapeDtypeStruct((act_rows, w), act_dtype),
            jax.ShapeDtypeStruct((nseq, 1, w), F32),
        ],
        scratch_shapes=[pltpu.VMEM((SUBLANES, w), F32), pltpu.VMEM((1, w), F32)],
        compiler_params=_cparams("arbitrary", "arbitrary"),
        name="lru_core",
    )(proj, proj, carry0, h0, conv_w, conv_b, wa, ba, wi, bi, lam)


def _sc_kernel(bg_ref, cg_ref, hh_ref, carry0_ref, cw_ref, act_ref, vt_ref, xp_ref, *, rows, n_chunks, t_valid,
               packed):
    c = pl.program_id(1)
    load = _loader(packed, t_valid)
    v = load(cg_ref) * load(hh_ref)
    u = _causal_conv(xp_ref, v, cw_ref, None, carry0_ref, SC_WIDTH, rows, c == 0)
    _store_act(act_ref, load(bg_ref) * u, packed, t_valid)
    lo = max(t_valid - SUBLANES, 0)

    @pl.when(c == n_chunks - 1)
    def _():
        vt_ref[...] = v[lo:lo + SUBLANES, :]


def _sc_core(proj, carry0, conv_w, *, nseq, rows, n_chunks, t_valid, packed_row0=None):
    kern = functools.partial(_sc_kernel, rows=rows, n_chunks=n_chunks, t_valid=t_valid,
                             packed=packed_row0 is not None)
    blk_rows, rmap, omap, act_rows, act_dtype = _row_blocks(nseq, rows, n_chunks, t_valid, packed_row0)
    d = D_MODEL
    return pl.pallas_call(
        kern,
        grid=(nseq, n_chunks),
        in_specs=[
            pl.BlockSpec((blk_rows, d), rmap),
            pl.BlockSpec((blk_rows, d), lambda b, c: (rmap(b, c)[0], 1)),
            pl.BlockSpec((blk_rows, d), lambda b, c: (rmap(b, c)[0], 2)),
            pl.BlockSpec((None, SUBLANES, d), lambda b, c: (b, 0, 0)),
            pl.BlockSpec((SC_WIDTH, d), lambda b, c: (0, 0)),
        ],
        out_specs=[
            pl.BlockSpec((blk_rows, d), omap),
            pl.BlockSpec((None, SUBLANES, d), lambda b, c: (b, 0, 0)),
        ],
        out_shape=[
            jax.ShapeDtypeStruct((act_rows, d), act_dtype),
            jax.ShapeDtypeStruct((nseq, SUBLANES, d), F32),
        ],
        scratch_shapes=[pltpu.VMEM((SUBLANES, d), F32)],
        compiler_params=_cparams("arbitrary", "arbitrary"),
        name="sconv_core",
    )(proj, proj, proj, carry0, conv_w)


def _route_kernel(lg_ref, info_ref, w12_ref, cnt_ref, carry_ref, *, tm, n_tiles):
    i = pl.program_id(0)

    @pl.when(i == 0)
    def _():
        carry_ref[...] = jnp.zeros_like(carry_ref)

    lg = lg_ref[...]
    lane = lax.broadcasted_iota(jnp.int32, (tm, LANES), 1).astype(F32)
    neg = -jnp.inf
    big = float(LANES)
    is_g = lane < MOE_GROUPS
    gl = jnp.where(is_g, lg, neg)
    gmax = jnp.max(gl, axis=1, keepdims=True)
    gidx = jnp.min(jnp.where(gl == gmax, lane, big), axis=1, keepdims=True)
    gsum = jnp.sum(jnp.where(is_g, jnp.exp(lg - gmax), 0.0), axis=1, keepdims=True)
    gw = 1.0 / gsum
    rel = lane - (MOE_GROUPS + MOE_PER_GROUP * gidx)
    in_grp = (rel >= 0.0) & (rel < MOE_PER_GROUP)
    el = jnp.where(in_grp, lg, neg)
    v1 = jnp.max(el, axis=1, keepdims=True)
    i1 = jnp.min(jnp.where(in_grp & (el == v1), lane, big), axis=1, keepdims=True)
    in_grp2 = in_grp & (lane != i1)
    el2 = jnp.where(in_grp2, lg, neg)
    v2 = jnp.max(el2, axis=1, keepdims=True)
    i2 = jnp.min(jnp.where(in_grp2 & (el2 == v2), lane, big), axis=1, keepdims=True)
    t = jnp.exp(v2 - v1)
    den = 1.0 + t
    w1 = (1.0 / den) * gw
    w2 = (t / den) * gw
    e1 = i1 - MOE_GROUPS
    e2 = i2 - MOE_GROUPS
    onehot = jnp.where(lane == e1, 1.0, jnp.where(lane == e2, 1.0, 0.0))
    r_i = lax.broadcasted_iota(jnp.int32, (tm, tm), 0)
    c_i = lax.broadcasted_iota(jnp.int32, (tm, tm), 1)
    tri = jnp.where(c_i < r_i, 1.0, 0.0).astype(BF16)
    before = _dot(tri, onehot.astype(BF16)) + carry_ref[0:1, :]
    r1 = jnp.sum(jnp.where(lane == e1, before, 0.0), axis=1, keepdims=True)
    r2 = jnp.sum(jnp.where(lane == e2, before, 0.0), axis=1, keepdims=True)
    carry_ref[0:1, :] = carry_ref[0:1, :] + jnp.sum(onehot, axis=0, keepdims=True)
    info_ref[...] = jnp.where(lane == 0, e1, jnp.where(lane == 1, e2, jnp.where(lane == 2, r1, jnp.where(
        lane == 3, r2, 0.0))))
    w12_ref[...] = jnp.concatenate(
        [jnp.broadcast_to(w1, (tm, LANES)), jnp.broadcast_to(w2, (tm, LANES))], axis=1)

    @pl.when(i == n_tiles - 1)
    def _():
        cnt_ref[...] = carry_ref[...]


def _route(logits, tm):
    n = logits.shape[0]
    n_tiles = n // tm
    kern = functools.partial(_route_kernel, tm=tm, n_tiles=n_tiles)
    return pl.pallas_call(
        kern,
        grid=(n_tiles,),
        in_specs=[pl.BlockSpec((tm, LANES), lambda i: (i, 0))],
        out_specs=[
            pl.BlockSpec((tm, LANES), lambda i: (i, 0)),
            pl.BlockSpec((tm, 2 * LANES), lambda i: (i, 0)),
            pl.BlockSpec((SUBLANES, LANES), lambda i: (0, 0)),
        ],
        out_shape=[
            jax.ShapeDtypeStruct((n, LANES), F32),
            jax.ShapeDtypeStruct((n, 2 * LANES), F32),
            jax.ShapeDtypeStruct((SUBLANES, LANES), F32),
        ],
        scratch_shapes=[pltpu.VMEM((SUBLANES, LANES), F32)],
        compiler_params=_cparams("arbitrary"),
        name="route",
    )(logits)


def _dispatch_kernel(d0_ref, d1_ref, x_ref, nw_ref, xg_ref, xn_ref, sem, *, tm):
    base = pl.program_id(0) * tm
    x = x_ref[...]
    ms = jnp.mean(x * x, axis=-1, keepdims=True)
    xn_ref[...] = x * lax.rsqrt(ms + RMS_EPS) * nw_ref[...]

    def row_copy(r, d):
        return pltpu.make_async_copy(xn_ref.at[pl.ds(r, 1), :], xg_ref.at[pl.ds(d, 1), :], sem)

    def issue(g, carry):
        r0 = pl.multiple_of(g * SUBLANES, SUBLANES)
        for j in range(SUBLANES):
            row_copy(r0 + j, d0_ref[base + r0 + j]).start(priority=0)
            row_copy(r0 + j, d1_ref[base + r0 + j]).start(priority=1)
        return carry

    lax.fori_loop(0, tm // SUBLANES, issue, 0)

    def drain(g, carry):
        for _ in range(2 * SUBLANES):
            row_copy(0, 0).wait()
        return carry

    lax.fori_loop(0, tm // SUBLANES, drain, 0)


def _dispatch(x1, norm_w, dest0, dest1, tm):
    n, d = x1.shape
    grid_spec = pltpu.PrefetchScalarGridSpec(
        num_scalar_prefetch=2,
        grid=(n // tm,),
        in_specs=[pl.BlockSpec((tm, d), lambda i, *_: (i, 0)), pl.BlockSpec((1, d), lambda i, *_: (0, 0))],
        out_specs=pl.BlockSpec(memory_space=pl.ANY),
        scratch_shapes=[pltpu.VMEM((tm, d), F32), pltpu.SemaphoreType.DMA(())],
    )
    return pl.pallas_call(
        functools.partial(_dispatch_kernel, tm=tm),
        grid_spec=grid_spec,
        out_shape=jax.ShapeDtypeStruct((2 * n, d), F32),
        compiler_params=_cparams("arbitrary"),
        name="moe_dispatch",
    )(dest0, dest1, x1, norm_w.reshape(1, d))


VISIT_VALID = 1
VISIT_FIRST_OF_EXPERT = 2
VISIT_SLOT = 4


def _ffn_kernel(vt_ref, ve_ref, vf_ref, vn_ref, off_ref, x_ref, wg_hbm, wu_hbm, wd_hbm, o_ref,
                sg_ref, su_ref, sd_ref, cg_ref, cu_ref, cd_ref, sem, *, tm, layer):
    v = pl.program_id(0)
    t = vt_ref[v]
    e = ve_ref[v]
    flags = vf_ref[v]
    first_of_tile = jnp.logical_or(v == 0, vt_ref[jnp.maximum(v - 1, 0)] != t)

    @pl.when(first_of_tile)
    def _():
        o_ref[...] = jnp.zeros_like(o_ref)

    def copies(expert, slot):
        return (pltpu.make_async_copy(wg_hbm.at[layer, expert], sg_ref.at[slot], sem.at[slot, 0]),
                pltpu.make_async_copy(wu_hbm.at[layer, expert], su_ref.at[slot], sem.at[slot, 1]),
                pltpu.make_async_copy(wd_hbm.at[layer, expert], sd_ref.at[slot], sem.at[slot, 2]))

    @pl.when((flags & VISIT_FIRST_OF_EXPERT) != 0)
    def _():
        slot = (flags & VISIT_SLOT) // VISIT_SLOT
        nxt = vn_ref[v]

        @pl.when(v == 0)
        def _():
            for c in copies(e, slot):
                c.start()

        @pl.when(nxt >= 0)
        def _():
            for c in copies(nxt, 1 - slot):
                c.start()

        for c in copies(e, slot):
            c.wait()
        cg_ref[...] = sg_ref[slot].astype(BF16)
        cu_ref[...] = su_ref[slot].astype(BF16)
        cd_ref[...] = sd_ref[slot].astype(BF16)

    @pl.when((flags & VISIT_VALID) != 0)
    def _():
        row = t * tm + lax.broadcasted_iota(jnp.int32, (tm, MOE_FF), 0)
        mine = (row >= off_ref[e]) & (row < off_ref[e + 1])
        x = x_ref[...].astype(BF16)
        hg = _dot(x, cg_ref[...])
        hu = _dot(x, cu_ref[...])
        hid = jnp.where(mine, _silu(hg) * hu, 0.0)
        o_ref[...] += _dot(hid.astype(BF16), cd_ref[...])


def _ffn(xg, w_gate, w_up, w_down, layer, plan, tm):
    rows, d = xg.shape
    n_visits = plan[0].shape[0]
    kern = functools.partial(_ffn_kernel, tm=tm, layer=layer)
    grid_spec = pltpu.PrefetchScalarGridSpec(
        num_scalar_prefetch=5,
        grid=(n_visits,),
        in_specs=[
            pl.BlockSpec((tm, d), lambda v, vt, *_: (vt[v], 0)),
            pl.BlockSpec(memory_space=pl.ANY),
            pl.BlockSpec(memory_space=pl.ANY),
            pl.BlockSpec(memory_space=pl.ANY),
        ],
        out_specs=pl.BlockSpec((tm, d), lambda v, vt, *_: (vt[v], 0)),
        scratch_shapes=[
            pltpu.VMEM((2, d, MOE_FF), F32), pltpu.VMEM((2, d, MOE_FF), F32), pltpu.VMEM((2, MOE_FF, d), F32),
            pltpu.VMEM((d, MOE_FF), BF16), pltpu.VMEM((d, MOE_FF), BF16), pltpu.VMEM((MOE_FF, d), BF16),
            pltpu.SemaphoreType.DMA((2, 3)),
        ],
    )
    return pl.pallas_call(
        kern,
        grid_spec=grid_spec,
        out_shape=jax.ShapeDtypeStruct((rows, d), F32),
        compiler_params=_cparams("arbitrary"),
        name="moe_ffn",
    )(*plan, xg, w_gate, w_up, w_down)


def _visit_plan(counts, n_tiles, tm):
    off = jnp.concatenate([jnp.zeros((1,), jnp.int32), jnp.cumsum(counts)])
    n_visits = n_tiles + N_EXPERTS - 1
    first_tile = off[:-1] // tm
    last_tile = jnp.where(counts > 0, (off[1:] - 1) // tm, first_tile)
    n_e = jnp.where(counts > 0, last_tile - first_tile + 1, 0)
    v_off = jnp.concatenate([jnp.zeros((1,), jnp.int32), jnp.cumsum(n_e)])
    total = v_off[-1]
    vid = jnp.arange(n_visits, dtype=jnp.int32)
    exp_of = jnp.sum((vid[:, None] >= v_off[None, 1:]).astype(jnp.int32), axis=1)
    exp_of = jnp.minimum(exp_of, N_EXPERTS - 1)
    tile_of = first_tile[exp_of] + (vid - v_off[exp_of])
    valid = vid < total
    first_of_expert = valid & (vid == v_off[exp_of])
    used = counts > 0
    order = jnp.cumsum(used.astype(jnp.int32)) - 1
    ids = jnp.arange(N_EXPERTS, dtype=jnp.int32)
    later = jnp.where(used[None, :] & (ids[None, :] > ids[:, None]), ids[None, :], N_EXPERTS)
    next_used = jnp.min(later, axis=1)
    next_used = jnp.where(next_used < N_EXPERTS, next_used, -1)
    last_e = jnp.minimum(jnp.sum((total - 1 >= v_off[1:]).astype(jnp.int32)), N_EXPERTS - 1)
    exp_of = jnp.where(valid, exp_of, last_e)
    tile_of = jnp.where(valid, tile_of, n_tiles - 1)
    flags = valid * VISIT_VALID + first_of_expert * VISIT_FIRST_OF_EXPERT + (order[exp_of] % 2) * VISIT_SLOT
    i32 = lambda a: a.astype(jnp.int32)
    return i32(tile_of), i32(exp_of), i32(flags), i32(next_used[exp_of]), i32(off)


def _combine_kernel(d0_ref, d1_ref, x1_ref, w12_ref, nw_ref, yg_ref, *refs, tm, n_tiles, n_prompt_tiles, final):
    outs, (buf, sem) = refs[:-2], refs[-2:]
    i = pl.program_id(0)

    def row_copy(slot, k, r, d):
        return pltpu.make_async_copy(yg_ref.at[pl.ds(d, 1), :], buf.at[slot, k, pl.ds(r, 1), :], sem.at[slot])

    def issue(tile, slot):
        base = tile * tm

        def body(g, carry):
            r0 = pl.multiple_of(g * SUBLANES, SUBLANES)
            for j in range(SUBLANES):
                row_copy(slot, 0, r0 + j, d0_ref[base + r0 + j]).start(priority=0)
                row_copy(slot, 1, r0 + j, d1_ref[base + r0 + j]).start(priority=1)
            return carry

        lax.fori_loop(0, tm // SUBLANES, body, 0)

    @pl.when(i == 0)
    def _():
        issue(0, 0)

    @pl.when(i + 1 < n_tiles)
    def _():
        issue(i + 1, lax.rem(i + 1, 2))

    slot = lax.rem(i, 2)

    def drain(g, carry):
        for _ in range(SUBLANES):
            row_copy(slot, 0, 0, 0).wait()
            row_copy(slot, 1, 0, 0).wait()
        return carry

    lax.fori_loop(0, tm // SUBLANES, drain, 0)

    w12 = w12_ref[...]
    reps = x1_ref.shape[1] // LANES
    w1 = jnp.concatenate([w12[:, :LANES]] * reps, axis=1)
    w2 = jnp.concatenate([w12[:, LANES:]] * reps, axis=1)
    x2 = x1_ref[...] + (w1 * buf[slot, 0] + w2 * buf[slot, 1])
    ms = jnp.mean(x2 * x2, axis=-1, keepdims=True)
    xn = x2 * lax.rsqrt(ms + RMS_EPS) * nw_ref[...]
    if final:
        split, split_val = outs, xn
    else:
        split, split_val = outs[:2], x2
        outs[2][...] = xn.astype(outs[2].dtype)

    @pl.when(i < n_prompt_tiles)
    def _():
        split[0][...] = split_val

    @pl.when(i >= n_prompt_tiles)
    def _():
        split[1][...] = split_val


def _combine(x1, w12, yg, dest0, dest1, norm_w, n_p, tm, final):
    n, d = x1.shape
    n_tiles = n // tm
    npt = n_p // tm
    spec_p = pl.BlockSpec((tm, d), lambda i, *_: (jnp.minimum(i, npt - 1), 0))
    spec_s = pl.BlockSpec((tm, d), lambda i, *_: (jnp.maximum(i - npt, 0), 0))
    out_specs = [spec_p, spec_s]
    out_shape = [jax.ShapeDtypeStruct((n_p, d), F32), jax.ShapeDtypeStruct((n - n_p, d), F32)]
    if not final:
        out_specs.append(pl.BlockSpec((tm, d), lambda i, *_: (i, 0)))
        out_shape.append(jax.ShapeDtypeStruct((n, d), BF16))
    grid_spec = pltpu.PrefetchScalarGridSpec(
        num_scalar_prefetch=2,
        grid=(n_tiles,),
        in_specs=[
            pl.BlockSpec((tm, d), lambda i, *_: (i, 0)),
            pl.BlockSpec((tm, 2 * LANES), lambda i, *_: (i, 0)),
            pl.BlockSpec((1, d), lambda i, *_: (0, 0)),
            pl.BlockSpec(memory_space=pl.ANY),
        ],
        out_specs=out_specs,
        scratch_shapes=[pltpu.VMEM((2, 2, tm, d), F32), pltpu.SemaphoreType.DMA((2,))],
    )
    return pl.pallas_call(
        functools.partial(_combine_kernel, tm=tm, n_tiles=n_tiles, n_prompt_tiles=npt, final=final),
        grid_spec=grid_spec,
        out_shape=out_shape,
        compiler_params=_cparams("arbitrary"),
        name="moe_combine",
    )(dest0, dest1, x1, w12, norm_w.reshape(1, d), yg)


def _pad_lanes(v, width=LANES):
    v = v.reshape(1, -1)
    return jnp.pad(v, ((0, 0), (0, width - v.shape[1])))


def _carry_rows(state, width):
    return jnp.pad(state, ((0, 0), (SUBLANES - (width - 1), 0), (0, 0)))


def _moe(x1, ffn_norm_w, logits, layer, w_gate, w_up, w_down, next_norm_w, n_p, final):
    n = x1.shape[0]
    info, w12, cnt = _route(logits, ROUTE_TM)
    counts = cnt[0, :N_EXPERTS].astype(jnp.int32)
    er = info[:, 0:4].T.astype(jnp.int32)
    plan = _visit_plan(counts, (2 * n) // FFN_TM, FFN_TM)
    ids = jnp.arange(N_EXPERTS, dtype=jnp.int32)[:, None]

    def first_row(e):
        return jnp.sum(jnp.where(ids < e[None, :], counts[:, None], 0), axis=0)

    dest0 = first_row(er[0]) + er[2]
    dest1 = first_row(er[1]) + er[3]
    xg = _dispatch(x1, ffn_norm_w, dest0, dest1, DISPATCH_TM)
    yg = _ffn(xg, w_gate, w_up, w_down, layer, plan, FFN_TM)
    return _combine(x1, w12, yg, dest0, dest1, next_norm_w, n_p, COMBINE_TM, final)


def kernel(x_prompt, x_sample, state_ssm_conv, state_ssm, state_lru_conv, state_lru, state_sconv, norm_mix, norm_ffn, norm_final, w_ssm_in, ssm_conv_w, ssm_conv_b, ssm_dt_bias, ssm_A_log, ssm_D, ssm_norm_w, w_ssm_out, w_lru_in, lru_conv_w, lru_conv_b, lru_wa, lru_ba, lru_wi, lru_bi, lru_lambda, w_lru_out, w_sc_in, sc_conv_w, w_sc_out, w_route_group, w_route_expert, w_exp_gate, w_exp_up, w_exp_down):
    b_p, t_p, d = x_prompt.shape
    b_s, t_s, _ = x_sample.shape
    n_p = b_p * t_p
    n_s = b_s * t_s
    n = n_p + n_s
    assert n_s == ROW_TM and n_p % ROW_TM == 0 and t_p % SSM_CHUNK == 0 and t_s <= SAMPLE_PAD_T
    mm_tm = n // 4
    n_chunks_p = t_p // SSM_CHUNK
    n_a = state_ssm.shape[0]

    x_p = x_prompt.reshape(n_p, d)
    x_s = x_sample.reshape(n_s, d)
    xn = _rmsnorm(x_p, x_s, norm_mix[0], ROW_TM, BF16)

    def last_rows_p(m, col0, ncol, k):
        return jnp.stack([m[(b + 1) * t_p - k:(b + 1) * t_p, col0:col0 + ncol] for b in range(b_p)])

    def last_rows_s(state, m, col0, ncol):
        new = m[n_p:, col0:col0 + ncol].reshape(b_s, t_s, ncol)
        return jnp.concatenate([state, new], axis=1)[:, t_s:]

    ssd_h_p = ssd_h_s = None
    ssd_h0_p = jnp.zeros((n_a * b_p, SSM_INNER, SSM_STATE), F32)
    ssd_h0_s = state_ssm.reshape(n_a * b_s, SSM_INNER, SSM_STATE)
    w_ssm_in_t = jnp.swapaxes(w_ssm_in, 1, 2)

    p_ssm_conv, p_lru_conv, p_lru, p_sconv = [], [], [], []
    s_ssm_conv, s_lru_conv, s_lru, s_sconv = [], [], [], []

    for i in range(DEPTH):
        kind = i % N_MIXERS
        s = i // N_MIXERS
        w_r = jnp.pad(jnp.concatenate([w_route_group[i], w_route_expert[i]], axis=1),
                      ((0, 0), (0, LANES - MOE_GROUPS - N_EXPERTS)))
        r_hi = w_r.astype(BF16)
        r_lo = (w_r - r_hi.astype(F32)).astype(BF16)
        if kind == 0:
            z = _matmul(xn, w_ssm_in_t, s, 0, SSM_INNER, PROJ_TN, mm_tm, transposed=True)
            xbc = _matmul(xn, w_ssm_in_t, s, SSM_INNER // PROJ_TN, SSM_CONV_DIM, PROJ_TN, mm_tm, transposed=True)
            w_dt = jnp.pad(w_ssm_in_t[s, SSM_INNER + SSM_CONV_DIM:, :], ((0, LANES - SSM_HEADS), (0, 0)))[None]
            dt_raw = _matmul(xn, w_dt, 0, 0, LANES, LANES, mm_tm, transposed=True)
            params = (ssm_conv_w[s], ssm_conv_b[s].reshape(1, -1), _pad_lanes(ssm_dt_bias[s]),
                      _pad_lanes(ssm_A_log[s]), jnp.repeat(ssm_D[s], SSM_HEAD_DIM).reshape(1, -1),
                      ssm_norm_w[s].reshape(1, -1))
            act_p, ssd_h_p = _ssd_core(
                z, xbc, dt_raw, jnp.zeros((b_p, SUBLANES, SSM_CONV_DIM), F32), ssd_h0_p, *params, ssd_h_p,
                nseq=b_p, rows=SSM_CHUNK, n_chunks=n_chunks_p, t_valid=SSM_CHUNK, seq0=s * b_p, n_slots=n_a * b_p)
            act_s, ssd_h_s = _ssd_core(
                z, xbc, dt_raw, _carry_rows(state_ssm_conv[s], SSM_CONV), ssd_h0_s, *params, ssd_h_s,
                nseq=b_s, rows=SAMPLE_PAD_T, n_chunks=1, t_valid=t_s, seq0=s * b_s, n_slots=n_a * b_s,
                packed_row0=n_p)
            p_ssm_conv.append(last_rows_p(xbc, 0, SSM_CONV_DIM, SSM_CONV - 1))
            s_ssm_conv.append(last_rows_s(state_ssm_conv[s], xbc, 0, SSM_CONV_DIM))
            w_out = w_ssm_out
        elif kind == 1:
            proj = _matmul(xn, w_lru_in, s, 0, 2 * LRU_WIDTH, PROJ_TN, mm_tm)
            params = (lru_conv_w[s], lru_conv_b[s].reshape(1, -1), lru_wa[s], lru_ba[s].reshape(1, -1), lru_wi[s],
                      lru_bi[s].reshape(1, -1), lru_lambda[s].reshape(1, -1))
            act_p, h_p = _lru_core(
                proj, jnp.zeros((b_p, SUBLANES, LRU_WIDTH), F32), jnp.zeros((b_p, 1, LRU_WIDTH), F32), *params,
                nseq=b_p, rows=SSM_CHUNK, n_chunks=n_chunks_p, t_valid=SSM_CHUNK)
            act_s, h_s = _lru_core(
                proj, _carry_rows(state_lru_conv[s], LRU_CONV), state_lru[s].reshape(b_s, 1, LRU_WIDTH), *params,
                nseq=b_s, rows=SAMPLE_PAD_T, n_chunks=1, t_valid=t_s, packed_row0=n_p)
            p_lru_conv.append(last_rows_p(proj, LRU_WIDTH, LRU_WIDTH, LRU_CONV - 1))
            s_lru_conv.append(last_rows_s(state_lru_conv[s], proj, LRU_WIDTH, LRU_WIDTH))
            p_lru.append(h_p.reshape(b_p, LRU_WIDTH))
            s_lru.append(h_s.reshape(b_s, LRU_WIDTH))
            w_out = w_lru_out
        else:
            proj = _matmul(xn, w_sc_in, s, 0, 3 * d, PROJ_TN, mm_tm)
            act_p, v_p = _sc_core(proj, jnp.zeros((b_p, SUBLANES, d), F32), sc_conv_w[s],
                                  nseq=b_p, rows=SSM_CHUNK, n_chunks=n_chunks_p, t_valid=SSM_CHUNK)
            act_s, v_s = _sc_core(proj, _carry_rows(state_sconv[s], SC_WIDTH), sc_conv_w[s],
                                  nseq=b_s, rows=SAMPLE_PAD_T, n_chunks=1, t_valid=t_s, packed_row0=n_p)
            p_sconv.append(v_p[:, SUBLANES - (SC_WIDTH - 1):])
            s_sconv.append(jnp.concatenate([state_sconv[s], v_s[:, :t_s]], axis=1)[:, t_s:])
            w_out = w_sc_out

        x1, logits = _outproj(act_p, act_s, w_out[s].astype(BF16), x_p, x_s, norm_ffn[i], r_hi, r_lo, PROJ_TK)
        moe_w = (w_exp_gate, w_exp_up, w_exp_down)
        if i < DEPTH - 1:
            x_p, x_s, xn = _moe(x1, norm_ffn[i], logits, i, *moe_w, norm_mix[i + 1], n_p, False)
        else:
            y_p, y_s = _moe(x1, norm_ffn[i], logits, i, *moe_w, norm_final, n_p, True)

    state_shape = (SSM_HEADS, SSM_HEAD_DIM, SSM_STATE)
    return (y_p.reshape(b_p, t_p, d), y_s.reshape(b_s, t_s, d),
            jnp.stack(p_ssm_conv), ssd_h_p.reshape((n_a, b_p) + state_shape), jnp.stack(p_lru_conv), jnp.stack(p_lru),
            jnp.stack(p_sconv),
            jnp.stack(s_ssm_conv), ssd_h_s.reshape((n_a, b_s) + state_shape), jnp.stack(s_lru_conv), jnp.stack(s_lru),
            jnp.stack(s_sconv))
```
